```python
import jax, jax.numpy as jnp
from jax import lax
import numpy as np

D_MODEL = 1024
BATCH = 8
SEQ = 8192
DEPTH = 4

N_A_LAYERS = DEPTH // 2
N_B_LAYERS = DEPTH - N_A_LAYERS
RMS_EPS = 1e-6
RWKV_HEAD = 64
RWKV_HEADS = D_MODEL // RWKV_HEAD
LORA_DECAY = 64
LORA_AAA = 64
LORA_MV = 32
LORA_GATE = 160
LNX_EPS = 64e-5
ATT_HEAD = 64
HEADS_PER_GROUP = D_MODEL // ATT_HEAD
DILATED_GROUPS = ((128, 1), (512, 4), (2048, 16))
N_GROUPS = len(DILATED_GROUPS)
BLOCK = 128
ROPE_THETA = 10000.0
Q_WIDTH = N_GROUPS * HEADS_PER_GROUP * ATT_HEAD
O_WIDTH = HEADS_PER_GROUP * ATT_HEAD
D_FF = -(-8 * D_MODEL // (3 * 256)) * 256

kernel_name = "yoco_rwkv7_dilated_window_hybrid"


def rms_norm(x, g):
    x32 = x.astype(jnp.float32)
    y = x32 * lax.rsqrt(jnp.mean(x32 * x32, axis=-1, keepdims=True) + RMS_EPS)
    return (y * g.astype(jnp.float32)).astype(x.dtype)


def rope_tables(seq):
    inv = ROPE_THETA ** (-jnp.arange(0, ATT_HEAD, 2, dtype=jnp.float32) / ATT_HEAD)
    ang = jnp.arange(seq, dtype=jnp.float32)[:, None] * inv[None, :]
    return jnp.cos(ang), jnp.sin(ang)


def apply_rope(t, cos, sin):
    c = cos[None, :, None, None, :]
    s = sin[None, :, None, None, :]
    t32 = t.astype(jnp.float32)
    t1, t2 = jnp.split(t32, 2, axis=-1)
    return jnp.concatenate([t1 * c - t2 * s, t2 * c + t1 * s], axis=-1).astype(t.dtype)


def swiglu_ffn(x, norm_g, w_gu, w_down):
    h = rms_norm(x, norm_g)
    gate, up = jnp.split(h @ w_gu, 2, axis=-1)
    return (jax.nn.silu(gate) * up) @ w_down


def wkv7_scan(r, w, k, v, a, b):
    Bn, Sn, Hn, Nn = r.shape

    def step(state, inp):
        r_t, w_t, k_t, v_t, a_t, b_t = inp
        sa = jnp.einsum('bhij,bhj->bhi', state, a_t)
        state = (state * w_t[:, :, None, :] + sa[..., None] * b_t[:, :, None, :]
                 + v_t[..., None] * k_t[:, :, None, :])
        return state, jnp.einsum('bhij,bhj->bhi', state, r_t)

    xs = tuple(jnp.moveaxis(t, 1, 0) for t in (r, w, k, v, a, b))
    state0 = jnp.zeros((Bn, Hn, Nn, Nn), jnp.float32)
    _, out = lax.scan(step, state0, xs)
    return jnp.moveaxis(out, 0, 1)


def rwkv7_time_mix(x, v_first, norm_g, mu, w_r, w_k, w_v, w0, w1, w2, a0, a1, a2,
                   g1, g2, k_k, k_a, r_k, lnx_g, lnx_b, w_o, v_mix):
    Bn, Sn, Dn = x.shape
    f32 = jnp.float32
    h = rms_norm(x, norm_g)
    dx = jnp.pad(h, ((0, 0), (1, 0), (0, 0)))[:, :-1] - h
    xr, xw, xk, xv, xa, xg = (h + dx * mu[i] for i in range(6))
    r = xr @ w_r
    w = -jax.nn.softplus(-(w0 + jnp.tanh(xw @ w1) @ w2)) - 0.5
    k = xk @ w_k
    v = xv @ w_v
    if v_mix is None:
        v_first = v
    else:
        v0, v1, v2 = v_mix
        v = v + (v_first - v) * jax.nn.sigmoid(v0 + (xv @ v1) @ v2)
    a = jax.nn.sigmoid(a0 + (xa @ a1) @ a2)
    g = jax.nn.sigmoid(xg @ g1) @ g2
    heads = lambda t: t.astype(f32).reshape(Bn, Sn, RWKV_HEADS, RWKV_HEAD)
    kk = heads(k * k_k)
    kk = kk / jnp.maximum(jnp.sqrt(jnp.sum(kk * kk, axis=-1, keepdims=True)), 1e-12)
    k = k * (1.0 + (a - 1.0) * k_a)
    rh, kh, vh, ah = heads(r), heads(k), heads(v), heads(a)
    decay = jnp.exp(-jnp.exp(heads(w)))
    o = wkv7_scan(rh, decay, kh, vh, -kk, kk * ah)
    mean = jnp.mean(o, axis=-1, keepdims=True)
    var = jnp.mean(jnp.square(o - mean), axis=-1, keepdims=True)
    o = (o - mean) * lax.rsqrt(var + LNX_EPS)
    o = o.reshape(Bn, Sn, Dn) * lnx_g.astype(f32) + lnx_b.astype(f32)
    bonus = jnp.sum(rh * kh * r_k.astype(f32), axis=-1, keepdims=True) * vh
    y = (o + bonus.reshape(Bn, Sn, Dn)) * g.astype(f32)
    return y.astype(x.dtype) @ w_o, v_first


def dilated_window_attention(q, k, v, window, dilation):
    Bn, Sn, Hn, Dh = q.shape
    span = window // dilation
    L = Sn // dilation
    nblk = -(-L // BLOCK)
    Lp = nblk * BLOCK

    def to_blocks(t):
        t = t.reshape(Bn, L, dilation, Hn, Dh).transpose(0, 2, 1, 3, 4)
        t = jnp.pad(t, ((0, 0), (0, 0), (0, Lp - L), (0, 0), (0, 0)))
        return t.reshape(Bn, dilation, nblk, BLOCK, Hn, Dh)

    def with_prev(t):
        prev = jnp.pad(t, ((0, 0), (0, 0), (1, 0), (0, 0), (0, 0), (0, 0)))[:, :, :-1]
        return jnp.concatenate([prev, t], axis=3)

    qb = to_blocks(q)
    kc = with_prev(to_blocks(k))
    vc = with_prev(to_blocks(v))
    s = jnp.einsum('brnqhd,brnkhd->brnhqk', qb, kc,
                   preferred_element_type=jnp.float32) * (Dh ** -0.5)
    qi = jnp.arange(BLOCK)[:, None]
    kj = jnp.arange(2 * BLOCK)[None, :]
    dist = qi + BLOCK - kj
    nb = jnp.arange(nblk)[:, None, None]
    valid = (dist >= 0) & (dist <= span) & ((nb > 0) | (kj >= BLOCK))
    s = jnp.where(valid[None, None, :, None, :, :], s, -jnp.inf)
    m = jnp.max(s, axis=-1, keepdims=True)
    p = jnp.exp(s - m)
    den = jnp.sum(p, axis=-1, keepdims=True)
    o = jnp.einsum('brnhqk,brnkhd->brnqhd', p, vc.astype(jnp.float32))
    o = o / jnp.swapaxes(den, 3, 4)
    lse = jnp.swapaxes((m + jnp.log(den))[..., 0], 3, 4)
    o = o.reshape(Bn, dilation, Lp, Hn, Dh)[:, :, :L].transpose(0, 2, 1, 3, 4).reshape(Bn, Sn, Hn, Dh)
    lse = lse.reshape(Bn, dilation, Lp, Hn)[:, :, :L].transpose(0, 2, 1, 3).reshape(Bn, Sn, Hn)
    return o, lse


def shared_kv(x, kv_norm_g, w_kv, k_norm_g, cos, sin):
    Bn, Sn, _ = x.shape
    h = rms_norm(x, kv_norm_g)
    kv = (h @ w_kv).reshape(Bn, Sn, 2, N_GROUPS, HEADS_PER_GROUP, ATT_HEAD)
    k = apply_rope(rms_norm(kv[:, :, 0], k_norm_g[:, None, :]), cos, sin)
    return k, kv[:, :, 1]


def dilated_mixer(x, k_sh, v_sh, cos, sin, norm_g, w_q, q_norm_g, w_o):
    Bn, Sn, _ = x.shape
    h = rms_norm(x, norm_g)
    q = (h @ w_q).reshape(Bn, Sn, N_GROUPS, HEADS_PER_GROUP, ATT_HEAD)
    q = apply_rope(rms_norm(q, q_norm_g[:, None, :]), cos, sin)
    outs, lses = [], []
    for gi, (window, dilation) in enumerate(DILATED_GROUPS):
        o, l = dilated_window_attention(q[:, :, gi], k_sh[:, :, gi], v_sh[:, :, gi], window, dilation)
        outs.append(o)
        lses.append(l)
    alpha = jax.nn.softmax(jnp.stack(lses), axis=0)
    o = jnp.einsum('gbsh,gbshd->bshd', alpha, jnp.stack(outs))
    return o.reshape(Bn, Sn, O_WIDTH).astype(x.dtype) @ w_o


def setup_inputs(seed: int = 0) -> dict:
    key = jax.random.key(seed)
    ks = iter(jax.random.split(key, 64))
    D, NA, NB = D_MODEL, N_A_LAYERS, N_B_LAYERS
    nrm = lambda shape, scale: jax.random.normal(next(ks), shape, jnp.float32) * scale
    gain = lambda shape: 1.0 + nrm(shape, 0.02)
    return {
        "x": nrm((BATCH, SEQ, D), 1.0),
        "a_norm_g": gain((NA, D)),
        "a_mu": jax.random.uniform(next(ks), (NA, 6, D), jnp.float32),
        "a_w_r": nrm((NA, D, D), D ** -0.5),
        "a_w_k": nrm((NA, D, D), D ** -0.5),
        "a_w_v": nrm((NA, D, D), D ** -0.5),
        "a_w0": jax.random.uniform(next(ks), (NA, D), jnp.float32, -6.5, -1.5),
        "a_w1": nrm((NA, D, LORA_DECAY), D ** -0.5),
        "a_w2": nrm((NA, LORA_DECAY, D), 0.1 * LORA_DECAY ** -0.5),
        "a_a0": nrm((NA, D), 0.1),
        "a_a1": nrm((NA, D, LORA_AAA), D ** -0.5),
        "a_a2": nrm((NA, LORA_AAA, D), 0.5 * LORA_AAA ** -0.5),
        "a_v0": nrm((NA - 1, D), 0.1),
        "a_v1": nrm((NA - 1, D, LORA_MV), D ** -0.5),
        "a_v2": nrm((NA - 1, LORA_MV, D), 0.5 * LORA_MV ** -0.5),
        "a_g1": nrm((NA, D, LORA_GATE), D ** -0.5),
        "a_g2": nrm((NA, LORA_GATE, D), LORA_GATE ** -0.5),
        "a_k_k": 0.85 + nrm((NA, D), 0.05),
        "a_k_a": 1.0 + nrm((NA, D), 0.05),
        "a_r_k": -0.04 + nrm((NA, RWKV_HEADS, RWKV_HEAD), 0.02),
        "a_lnx_g": gain((NA, D)),
        "a_lnx_b": nrm((NA, D), 0.01),
        "a_w_o": nrm((NA, D, D), D ** -0.5),
        "kv_norm_g": gain((D,)),
        "w_kv": nrm((D, 2 * Q_WIDTH), D ** -0.5),
        "k_norm_g": gain((N_GROUPS, ATT_HEAD)),
        "b_norm_g": gain((NB, D)),
        "b_w_q": nrm((NB, D, Q_WIDTH), D ** -0.5),
        "b_q_norm_g": gain((NB, N_GROUPS, ATT_HEAD)),
        "b_w_o": nrm((NB, O_WIDTH, D), O_WIDTH ** -0.5),
        "f_norm_g": gain((DEPTH, D)),
        "f_w_gu": nrm((DEPTH, D, 2 * D_FF), D ** -0.5),
        "f_w_down": nrm((DEPTH, D_FF, D), D_FF ** -0.5),
    }


def reference(x, a_norm_g, a_mu, a_w_r, a_w_k, a_w_v, a_w0, a_w1, a_w2, a_a0, a_a1, a_a2,
              a_v0, a_v1, a_v2, a_g1, a_g2, a_k_k, a_k_a, a_r_k, a_lnx_g, a_lnx_b, a_w_o,
              kv_norm_g, w_kv, k_norm_g, b_norm_g, b_w_q, b_q_norm_g, b_w_o,
              f_norm_g, f_w_gu, f_w_down):
    cos, sin = rope_tables(x.shape[1])
    v_first = None
    k_sh = v_sh = None
    for layer in range(DEPTH):
        if layer < N_A_LAYERS:
            i = layer
            v_mix = None if i == 0 else (a_v0[i - 1], a_v1[i - 1], a_v2[i - 1])
            y, v_first = rwkv7_time_mix(
                x, v_first, a_norm_g[i], a_mu[i], a_w_r[i], a_w_k[i], a_w_v[i],
                a_w0[i], a_w1[i], a_w2[i], a_a0[i], a_a1[i], a_a2[i], a_g1[i], a_g2[i],
                a_k_k[i], a_k_a[i], a_r_k[i], a_lnx_g[i], a_lnx_b[i], a_w_o[i], v_mix)
            x = x + y
        else:
            j = layer - N_A_LAYERS
            x = x + dilated_mixer(x, k_sh, v_sh, cos, sin, b_norm_g[j], b_w_q[j],
                                  b_q_norm_g[j], b_w_o[j])
        x = x + swiglu_ffn(x, f_norm_g[layer], f_w_gu[layer], f_w_down[layer])
        if layer == N_A_LAYERS - 1:
            k_sh, v_sh = shared_kv(x, kv_norm_g, w_kv, k_norm_g, cos, sin)
    return x
```

```python
import functools

import jax
import jax.numpy as jnp
from jax import lax
from jax.experimental import pallas as pl
from jax.experimental.pallas import tpu as pltpu

F32 = jnp.float32
BF = jnp.bfloat16

D_MODEL = 1024
RMS_EPS = 1e-6
LNX_EPS = 64e-5
HEAD = 64
LANES = 128
N_PAIRS = D_MODEL // LANES
CHUNK = 64
DILATED_GROUPS = ((128, 1), (512, 4), (2048, 16))
N_GROUPS = len(DILATED_GROUPS)
ATT_BLOCK = 128
ROPE_THETA = 10000.0
Q_WIDTH = N_GROUPS * D_MODEL
D_FF = 2816
FF_CHUNK = 1408
NEG_BIG = -1e30
DECAY_SCALE = 0.6065306597126334

ROW_TILE = 256
WKV_TILE = 256
VMEM_LIMIT = 56 * 1024 * 1024


def _iota(shape, dim):
    return lax.broadcasted_iota(jnp.int32, shape, dim)


def _const_spec(shape):
    nd = len(shape)
    return pl.BlockSpec(shape, lambda *_: (0,) * nd, pipeline_mode=pl.Buffered(1))


def _params(sem):
    return pltpu.CompilerParams(dimension_semantics=sem, vmem_limit_bytes=VMEM_LIMIT)


def _bdot(a, b):
    return jnp.dot(a.astype(BF), b.astype(BF), preferred_element_type=F32)


def _bdot_nt(a, b):
    return lax.dot_general(a.astype(BF), b.astype(BF), (((1,), (1,)), ((), ())),
                           preferred_element_type=F32)


def _bdot_tn(a, b):
    return lax.dot_general(a.astype(BF), b.astype(BF), (((0,), (0,)), ((), ())),
                           preferred_element_type=F32)


def _rms(x, g):
    ms = jnp.mean(x * x, axis=-1, keepdims=True)
    return x * lax.rsqrt(ms + RMS_EPS) * g


def _hi_lo(x):
    hi = x.astype(BF)
    lo = (x - hi.astype(F32)).astype(BF)
    return hi, lo


def _rwkv_proj_body(has_vmix, tiles_per_seq, *refs):
    if has_vmix:
        (x_ref, vf_ref, ng_ref, mu_ref, wr_ref, wk_ref, wv_ref, w0_ref, w1_ref, w2_ref,
         a0_ref, a1_ref, a2_ref, v0_ref, v1_ref, v2_ref, g1_ref, g2_ref,
         r_out, k_out, v_out, ld_out, al_out, g_out, carry_ref) = refs
    else:
        (x_ref, ng_ref, mu_ref, wr_ref, wk_ref, wv_ref, w0_ref, w1_ref, w2_ref,
         a0_ref, a1_ref, a2_ref, g1_ref, g2_ref,
         r_out, k_out, v_out, ld_out, al_out, g_out, carry_ref) = refs
    i = pl.program_id(0)
    tm = x_ref.shape[0]
    h = _rms(x_ref[...], ng_ref[...])

    @pl.when(i % tiles_per_seq == 0)
    def _():
        carry_ref[...] = jnp.zeros_like(carry_ref)

    prev_last = carry_ref[7:8, :]
    hprev = jnp.where(_iota(h.shape, 0) == 0, prev_last, pltpu.roll(h, 1, 0))
    carry_ref[...] = h[tm - 8:tm, :]
    dx = hprev - h
    mu = mu_ref[...]
    mix = lambda j: (h + dx * mu[j:j + 1, :]).astype(BF)
    xr, xw, xk, xv, xa, xg = (mix(j) for j in range(6))

    r_out[...] = _bdot(xr, wr_ref[...]).astype(BF)
    k_out[...] = _bdot(xk, wk_ref[...])
    v = _bdot(xv, wv_ref[...])
    if has_vmix:
        gate = jax.nn.sigmoid(v0_ref[...] + _bdot(_bdot(xv, v1_ref[...]), v2_ref[...]))
        v = v + (vf_ref[...].astype(F32) - v) * gate
    v_out[...] = v.astype(BF)
    z = w0_ref[...] + _bdot(jnp.tanh(_bdot(xw, w1_ref[...])), w2_ref[...])
    ld_out[...] = -DECAY_SCALE * jax.nn.sigmoid(z)
    al_out[...] = jax.nn.sigmoid(a0_ref[...] + _bdot(_bdot(xa, a1_ref[...]), a2_ref[...]))
    g_out[...] = _bdot(jax.nn.sigmoid(_bdot(xg, g1_ref[...])), g2_ref[...]).astype(BF)


def _rwkv_proj(x2d, v_first, seq, ng, mu, wr, wk, wv, w0, w1, w2, a0, a1, a2, vmix, g1, g2):
    m, d = x2d.shape
    tm = ROW_TILE
    has_vmix = vmix is not None
    row = pl.BlockSpec((tm, d), lambda i: (i, 0))
    ins = [x2d] + ([v_first] if has_vmix else []) + [ng, mu, wr, wk, wv, w0, w1, w2, a0, a1, a2]
    ins += (list(vmix) if has_vmix else []) + [g1, g2]
    n_row = 2 if has_vmix else 1
    specs = [row] * n_row + [_const_spec(a.shape) for a in ins[n_row:]]
    out_dt = (BF, F32, BF, F32, F32, BF)
    return pl.pallas_call(
        functools.partial(_rwkv_proj_body, has_vmix, seq // tm),
        grid=(m // tm,),
        in_specs=specs,
        out_specs=[row] * 6,
        out_shape=[jax.ShapeDtypeStruct((m, d), t) for t in out_dt],
        scratch_shapes=[pltpu.VMEM((8, d), F32)],
        compiler_params=_params(("arbitrary",)),
        name="rwkv_proj",
    )(*ins)


def _wkv_masks():
    c = CHUNK
    row = _iota((c, LANES), 0)
    rr = _iota((c, LANES), 1) & (c - 1)
    r128 = _iota((LANES, LANES), 0)
    l128 = _iota((LANES, LANES), 1)
    m = {}
    m["strict"] = row > rr
    m["incl"] = row >= rr
    m["d16"] = (row >> 4) == (rr >> 4)
    m["off1"] = ((row >> 5) == (rr >> 5)) & (((row >> 4) & 1) == 1) & (((rr >> 4) & 1) == 0)
    m["off2"] = ((row >> 5) == 1) & ((rr >> 5) == 0)
    m["bd16"] = (r128 >> 4) == (l128 >> 4)
    m["bd32"] = (r128 >> 5) == (l128 >> 5)
    m["bd64"] = (r128 >> 6) == (l128 >> 6)
    m["eye"] = r128 == l128
    m["eye16"] = _iota((16, LANES), 0) == (_iota((16, LANES), 1) & 15)
    m["c32"] = (_iota((32, LANES), 0) >> 4) == ((_iota((32, LANES), 1) & 31) >> 4)
    m["c64"] = (_iota((c, LANES), 0) >> 5) == (rr >> 5)
    m["ltri"] = jnp.where(_iota((c, c), 0) >= _iota((c, c), 1), 1.0, 0.0).astype(BF)
    seg = jnp.where(m["bd64"], 1.0, 0.0).astype(BF)
    m["seg"] = jnp.concatenate([seg, seg], axis=0)
    return m


def _seg_sum(x, m):
    hi, lo = _hi_lo(x)
    return jnp.dot(jnp.concatenate([hi, lo], axis=1), m["seg"], preferred_element_type=F32)


def _tile_rows(x, reps, mask):
    return jnp.where(mask, jnp.concatenate([x] * reps, axis=0), 0.0)


def _wkv_chunk_ops(r, ld, k, v, a, b, m):
    c = CHUNK
    h1 = ld.astype(BF)
    r1 = ld - h1.astype(F32)
    h2 = r1.astype(BF)
    h3 = (r1 - h2.astype(F32)).astype(BF)
    cum3 = jnp.dot(m["ltri"], jnp.concatenate([h1, h2, h3], axis=1), preferred_element_type=F32)
    cum = cum3[:, :LANES] + cum3[:, LANES:2 * LANES] + cum3[:, 2 * LANES:]
    mid = cum[c // 2 - 1:c // 2, :]
    tot = cum[c - 1:c, :]
    e_in = jnp.exp(cum - mid)
    e_ex = jnp.exp(cum - ld - mid)
    e_inv = jnp.exp(mid - cum)
    e_mid = jnp.exp(mid)
    e_tm = jnp.exp(tot - mid)
    at, rt, bt, kt = a * e_ex, r * e_in, b * e_inv, k * e_inv
    a0, r0, bh, kh = at * e_mid, rt * e_mid, bt * e_tm, kt * e_tm

    stack = lambda x: _tile_rows(x, 2, m["bd64"])
    quad = _bdot_nt(jnp.concatenate([at, rt], axis=0),
                    jnp.concatenate([stack(bt), stack(kt)], axis=0))
    q_ab, q_ak = quad[:c, :LANES], quad[:c, LANES:]
    aab = jnp.where(m["strict"], q_ab, 0.0)
    aak = jnp.where(m["strict"], q_ak, 0.0)
    mrb = jnp.where(m["incl"], quad[c:, :LANES], 0.0)
    mrk = jnp.where(m["incl"], quad[c:, LANES:], 0.0)

    ad = jnp.where(m["d16"], q_ab, 0.0)
    ad = jnp.where(m["strict"], ad, 0.0)
    pc = ad[0:16] + ad[16:32] + ad[32:48] + ad[48:64]
    pf = _tile_rows(pc, 8, m["bd16"])
    t16 = pc + jnp.where(m["eye16"], 1.0, 0.0)
    for _ in range(3):
        pc = _bdot(pc, pf)
        pf = _tile_rows(pc, 8, m["bd16"])
        t16 = t16 + _bdot(t16, pf)
    t32d = _tile_rows(t16, 2, m["c32"])
    a1 = jnp.where(m["off1"], q_ab, 0.0)
    a1c = a1[0:32] + a1[32:64]
    n1 = _bdot(a1c, _tile_rows(t32d, 4, m["bd32"]))
    t32 = t32d + _bdot(t32d, _tile_rows(n1, 4, m["bd32"]))
    t64d = _tile_rows(t32, 2, m["c64"])
    a2c = jnp.where(m["off2"], q_ab, 0.0)
    n2 = _bdot(a2c, _tile_rows(t64d, 2, m["bd64"]))
    tc = t64d + _bdot(t64d, _tile_rows(n2, 2, m["bd64"]))

    vst = stack(v)
    x1 = _bdot(aak, vst)
    wu = _bdot(tc, jnp.concatenate([stack(a0), stack(x1)], axis=1))
    w, u0 = wu[:, :LANES], wu[:, LANES:]
    qp = r0 + _bdot(mrb, stack(w))
    o0 = _bdot(jnp.concatenate([mrb, mrk], axis=1), jnp.concatenate([stack(u0), vst], axis=0))
    gh = _bdot_tn(jnp.concatenate([bh, kh], axis=0),
                  jnp.concatenate([jnp.concatenate([w, u0], axis=1),
                                   jnp.concatenate([jnp.zeros_like(v), v], axis=1)], axis=0))
    g = jnp.where(m["bd64"], gh[:, :LANES], 0.0) + jnp.where(m["eye"], jnp.exp(tot), 0.0)
    hh = jnp.where(m["bd64"], gh[:, LANES:], 0.0)
    return qp, o0, g, hh


def _wkv_body(r_ref, k_ref, v_ref, ld_ref, al_ref, g_ref, kk_ref, ka_ref, rk_ref, lg_ref, lb_ref,
              y_ref, state_ref):
    t = pl.program_id(2)

    @pl.when(t == 0)
    def _():
        state_ref[...] = jnp.zeros_like(state_ref)

    m = _wkv_masks()
    n_chunks = r_ref.shape[1] // CHUNK
    k_k, k_a, r_k = kk_ref[...], ka_ref[...], rk_ref[...]
    lnx_g, lnx_b = lg_ref[...], lb_ref[...]
    pre = []
    for ci in range(n_chunks):
        rows = pl.ds(ci * CHUNK, CHUNK)
        r = r_ref[0, rows, :].astype(F32)
        k_raw = k_ref[0, rows, :]
        v = v_ref[0, rows, :].astype(F32)
        ld = ld_ref[0, rows, :]
        al = al_ref[0, rows, :]
        kkx = k_raw * k_k
        nrm = jnp.sqrt(_seg_sum(kkx * kkx, m))
        kk = kkx / jnp.maximum(nrm, 1e-12)
        k = k_raw * (1.0 + (al - 1.0) * k_a)
        ops = _wkv_chunk_ops(r, ld, k, v, -kk, kk * al, m)
        bonus = _seg_sum(r * k * r_k, m) * v
        pre.append(ops + (bonus,))

    state = state_ref[...]
    for ci in range(n_chunks):
        qp, o0, g, hh, bonus = pre[ci]
        rows = pl.ds(ci * CHUNK, CHUNK)
        o = _bdot(qp, state) + o0
        state = _bdot(g, state) + hh
        mean = _seg_sum(o, m) * (1.0 / HEAD)
        dev = o - mean
        var = _seg_sum(dev * dev, m) * (1.0 / HEAD)
        y = dev * lax.rsqrt(var + LNX_EPS) * lnx_g + lnx_b + bonus
        y_ref[0, rows, :] = (y * g_ref[0, rows, :].astype(F32)).astype(BF)
    state_ref[...] = state


def _wkv(r, k, v, ld, al, g, k_k, k_a, r_k, lnx_g, lnx_b):
    b, s, d = r.shape
    tc = WKV_TILE
    blk = pl.BlockSpec((1, tc, LANES), lambda bi, hp, t: (bi, t, hp))
    par = pl.BlockSpec((1, LANES), lambda bi, hp, t: (0, hp))
    return pl.pallas_call(
        _wkv_body,
        grid=(b, d // LANES, s // tc),
        in_specs=[blk] * 6 + [par] * 5,
        out_specs=blk,
        out_shape=jax.ShapeDtypeStruct((b, s, d), BF),
        scratch_shapes=[pltpu.VMEM((LANES, LANES), F32)],
        compiler_params=_params(("parallel", "parallel", "arbitrary")),
        name="wkv",
    )(r, k, v, ld, al, g, k_k, k_a, r_k, lnx_g, lnx_b)


def _ffn(x1, fng, wgu_ref, wdown_ref):
    h = _rms(x1, fng).astype(BF)
    acc = x1
    for c in range(D_FF // FF_CHUNK):
        lo = c * FF_CHUNK
        gate = jnp.dot(h, wgu_ref[:, lo:lo + FF_CHUNK], preferred_element_type=F32)
        up = jnp.dot(h, wgu_ref[:, D_FF + lo:D_FF + lo + FF_CHUNK], preferred_element_type=F32)
        act = (gate * jax.nn.sigmoid(gate) * up).astype(BF)
        acc = acc + jnp.dot(act, wdown_ref[lo:lo + FF_CHUNK, :], preferred_element_type=F32)
    return acc


def _oproj_ffn_body(x_ref, y_ref, wo_ref, fng_ref, wgu_ref, wdown_ref, out_ref):
    x1 = x_ref[...] + jnp.dot(y_ref[...], wo_ref[...], preferred_element_type=F32)
    out_ref[...] = _ffn(x1, fng_ref[...], wgu_ref, wdown_ref)


def _oproj_ffn(x2d, y2d, wo, fng, wgu, wdown):
    m, d = x2d.shape
    tm = ROW_TILE
    row = pl.BlockSpec((tm, d), lambda i: (i, 0))
    consts = [wo, fng, wgu, wdown]
    return pl.pallas_call(
        _oproj_ffn_body,
        grid=(m // tm,),
        in_specs=[row, row] + [_const_spec(a.shape) for a in consts],
        out_specs=row,
        out_shape=jax.ShapeDtypeStruct((m, d), F32),
        compiler_params=_params(("parallel",)),
        name="oproj_ffn",
    )(x2d, y2d, *consts)


def _merge_oproj_ffn_body(x_ref, o1_ref, o2_ref, o3_ref, l1_ref, l2_ref, l3_ref, ex_ref,
                          wo_ref, fng_ref, wgu_ref, wdown_ref, out_ref):
    l1, l2, l3 = l1_ref[...], l2_ref[...], l3_ref[...]
    mx = jnp.maximum(jnp.maximum(l1, l2), l3)
    e1, e2, e3 = jnp.exp(l1 - mx), jnp.exp(l2 - mx), jnp.exp(l3 - mx)
    inv = 1.0 / (e1 + e2 + e3)
    o = jnp.zeros(x_ref.shape, F32)
    for e, o_ref in ((e1, o1_ref), (e2, o2_ref), (e3, o3_ref)):
        hi, lo = _hi_lo(e * inv)
        alpha = jnp.dot(jnp.concatenate([hi, lo], axis=1), ex_ref[...], preferred_element_type=F32)
        o = o + alpha * o_ref[...].astype(F32)
    x1 = x_ref[...] + _bdot(o, wo_ref[...])
    out_ref[...] = _ffn(x1, fng_ref[...], wgu_ref, wdown_ref)


def _merge_oproj_ffn(x2d, os_, ls_, expand, wo, fng, wgu, wdown):
    m, d = x2d.shape
    tm = ROW_TILE
    row = pl.BlockSpec((tm, d), lambda i: (i, 0))
    lrow = pl.BlockSpec((tm, LANES), lambda i: (i, 0))
    consts = [expand, wo, fng, wgu, wdown]
    return pl.pallas_call(
        _merge_oproj_ffn_body,
        grid=(m // tm,),
        in_specs=[row] * 4 + [lrow] * 3 + [_const_spec(a.shape) for a in consts],
        out_specs=row,
        out_shape=jax.ShapeDtypeStruct((m, d), F32),
        compiler_params=_params(("parallel",)),
        name="merge_oproj_ffn",
    )(x2d, *os_, *ls_, *consts)


PROJ_CHUNK = 512


def _head_proj_body(n_rope, scale, x_ref, ng_ref, w_ref, hg_ref, cos_ref, sin_ref, seg_ref, out_ref):
    h = _rms(x_ref[...], ng_ref[...]).astype(BF)
    cosb, sinb = cos_ref[...], sin_ref[...]
    first_half = (_iota(cosb.shape, 1) & (HEAD - 1)) < HEAD // 2
    n_out = out_ref.shape[1]
    for c in range(n_out // PROJ_CHUNK):
        lo = c * PROJ_CHUNK
        y = jnp.dot(h, w_ref[:, lo:lo + PROJ_CHUNK], preferred_element_type=F32)
        if lo >= n_rope:
            out_ref[:, lo:lo + PROJ_CHUNK] = y.astype(BF)
            continue
        for j in range(PROJ_CHUNK // LANES):
            yb = y[:, j * LANES:(j + 1) * LANES]
            ss = jnp.dot((yb * yb).astype(BF), seg_ref[...], preferred_element_type=F32)
            col = lo + j * LANES
            yn = yb * lax.rsqrt(ss * (1.0 / HEAD) + RMS_EPS) * hg_ref[:, col:col + LANES]
            swapped = jnp.where(first_half, pltpu.roll(yn, LANES - HEAD // 2, 1),
                                pltpu.roll(yn, HEAD // 2, 1))
            out_ref[:, col:col + LANES] = ((yn * cosb + swapped * sinb) * scale).astype(BF)


def _head_proj(x2d, seq, ng, w, head_gain, n_rope, scale, cos_t, sin_t, seg):
    m, d = x2d.shape
    n_out = w.shape[1]
    tm = ROW_TILE
    tiles_per_seq = seq // tm
    row = pl.BlockSpec((tm, d), lambda i: (i, 0))
    tab = pl.BlockSpec((tm, LANES), lambda i: (i % tiles_per_seq, 0))
    return pl.pallas_call(
        functools.partial(_head_proj_body, n_rope, scale),
        grid=(m // tm,),
        in_specs=[row, _const_spec(ng.shape), _const_spec(w.shape), _const_spec(head_gain.shape),
                  tab, tab, _const_spec(seg.shape)],
        out_specs=pl.BlockSpec((tm, n_out), lambda i: (i, 0)),
        out_shape=jax.ShapeDtypeStruct((m, n_out), BF),
        compiler_params=_params(("parallel",)),
        name="head_proj",
    )(x2d, ng, w, head_gain, cos_t, sin_t, seg)


def _attn_body(q_ref, kp_ref, kc_ref, vp_ref, vc_ref, o_ref, l_ref):
    n = pl.program_id(2)
    blk = ATT_BLOCK
    row = _iota((blk, 2 * blk), 0)
    col = _iota((blk, 2 * blk), 1)
    prev_bias = jnp.where(n > 0, 0.0, NEG_BIG)
    bias = jnp.where(col >= blk, jnp.where(col - blk <= row, 0.0, NEG_BIG),
                     jnp.where(col >= row, prev_bias, NEG_BIG))
    lane = _iota((blk, LANES), 1)
    lse_tile = jnp.zeros((blk, LANES), F32)
    for j in range(q_ref.shape[2] // LANES):
        cols = slice(j * LANES, (j + 1) * LANES)
        q = q_ref[0, :, cols]
        kcat = jnp.concatenate([kp_ref[0, :, cols], kc_ref[0, :, cols]], axis=0)
        vcat = jnp.concatenate([vp_ref[0, :, cols], vc_ref[0, :, cols]], axis=0)
        outs = []
        for hh in range(2):
            in_head = (lane >= hh * HEAD) & (lane < (hh + 1) * HEAD)
            qm = jnp.where(in_head, q, jnp.zeros_like(q))
            s = lax.dot_general(qm, kcat, (((1,), (1,)), ((), ())), preferred_element_type=F32) + bias
            mx = jnp.max(s, axis=-1, keepdims=True)
            p = jnp.exp(s - mx)
            den = jnp.sum(p, axis=-1, keepdims=True)
            pv = jnp.dot(p.astype(BF), vcat, preferred_element_type=F32)
            outs.append(pv / den)
            lse_tile = jnp.where(lane == 2 * j + hh, mx + jnp.log(den), lse_tile)
        o_ref[0, :, cols] = jnp.where(lane < HEAD, outs[0], outs[1]).astype(BF)
    l_ref[0] = lse_tile


def _attn(q, kv, gi, dilation):
    b, s, _ = q.shape
    d = D_MODEL
    ln = s // dilation
    nblk = ln // ATT_BLOCK
    qv = q.reshape(b, ln, dilation * Q_WIDTH)
    kvv = kv.reshape(b, ln, dilation * 2 * Q_WIDTH)
    blk = ATT_BLOCK
    q_spec = pl.BlockSpec((1, blk, d), lambda bi, r, n: (bi, n, r * N_GROUPS + gi))
    kc = pl.BlockSpec((1, blk, d), lambda bi, r, n: (bi, n, r * 2 * N_GROUPS + gi))
    kp = pl.BlockSpec((1, blk, d), lambda bi, r, n: (bi, jnp.maximum(n - 1, 0), r * 2 * N_GROUPS + gi))
    vc = pl.BlockSpec((1, blk, d), lambda bi, r, n: (bi, n, r * 2 * N_GROUPS + N_GROUPS + gi))
    vp = pl.BlockSpec((1, blk, d),
                      lambda bi, r, n: (bi, jnp.maximum(n - 1, 0), r * 2 * N_GROUPS + N_GROUPS + gi))
    o, lse = pl.pallas_call(
        _attn_body,
        grid=(b, dilation, nblk),
        in_specs=[q_spec, kp, kc, vp, vc],
        out_specs=[pl.BlockSpec((1, blk, d), lambda bi, r, n: (bi, n, r)),
                   pl.BlockSpec((1, blk, LANES), lambda bi, r, n: (bi, n, r))],
        out_shape=[jax.ShapeDtypeStruct((b, ln, dilation * d), BF),
                   jax.ShapeDtypeStruct((b, ln, dilation * LANES), F32)],
        compiler_params=_params(("parallel", "parallel", "arbitrary")),
        name=f"attn_d{dilation}",
    )(qv, kvv, kvv, kvv, kvv)
    return o.reshape(b * s, d), lse.reshape(b * s, LANES)


def _rope_tables(seq):
    inv = ROPE_THETA ** (-jnp.arange(0, HEAD, 2, dtype=F32) / HEAD)
    ang = jnp.arange(seq, dtype=F32)[:, None] * inv[None, :]
    cos, sin = jnp.cos(ang), jnp.sin(ang)
    cos_t = jnp.tile(cos, (1, LANES // (HEAD // 2)))
    sin_t = jnp.tile(jnp.concatenate([-sin, sin], axis=1), (1, LANES // HEAD))
    return cos_t, sin_t


def kernel(x, a_norm_g, a_mu, a_w_r, a_w_k, a_w_v, a_w0, a_w1, a_w2, a_a0, a_a1, a_a2, a_v0, a_v1, a_v2, a_g1, a_g2, a_k_k, a_k_a, a_r_k, a_lnx_g, a_lnx_b, a_w_o, kv_norm_g, w_kv, k_norm_g, b_norm_g, b_w_q, b_q_norm_g, b_w_o, f_norm_g, f_w_gu, f_w_down):
    bsz, seq, d = x.shape
    assert d == D_MODEL and seq % (ATT_BLOCK * DILATED_GROUPS[-1][1]) == 0
    assert seq % ROW_TILE == 0 and seq % WKV_TILE == 0
    n_a = a_norm_g.shape[0]
    n_b = b_norm_g.shape[0]
    row1 = lambda t: t.reshape(1, -1).astype(F32)
    bf = lambda t: t.astype(BF)

    cos_t, sin_t = _rope_tables(seq)
    seg128 = jnp.where((_iota((LANES, LANES), 0) >> 6) == (_iota((LANES, LANES), 1) >> 6), 1.0, 0.0).astype(BF)
    ex = jnp.where(_iota((LANES, d), 0) == (_iota((LANES, d), 1) >> 6), 1.0, 0.0).astype(BF)
    expand = jnp.concatenate([ex, ex], axis=0)

    x2d = x.reshape(bsz * seq, d)
    v_first = None
    for i in range(n_a):
        vmix = None if i == 0 else (row1(a_v0[i - 1]), bf(a_v1[i - 1]), bf(a_v2[i - 1]))
        mu = jnp.pad(a_mu[i], ((0, 2), (0, 0)))
        r, k, v, ld, al, g = _rwkv_proj(
            x2d, v_first, seq, row1(a_norm_g[i]), mu, bf(a_w_r[i]), bf(a_w_k[i]), bf(a_w_v[i]),
            row1(a_w0[i]), bf(a_w1[i]), bf(a_w2[i]), row1(a_a0[i]), bf(a_a1[i]), bf(a_a2[i]),
            vmix, bf(a_g1[i]), bf(a_g2[i]))
        if i == 0:
            v_first = v
        to3 = lambda t: t.reshape(bsz, seq, d)
        y = _wkv(to3(r), to3(k), to3(v), to3(ld), to3(al), to3(g), row1(a_k_k[i]), row1(a_k_a[i]),
                 row1(a_r_k[i]), row1(a_lnx_g[i]), row1(a_lnx_b[i]))
        x2d = _oproj_ffn(x2d, y.reshape(bsz * seq, d), bf(a_w_o[i]), row1(f_norm_g[i]),
                         bf(f_w_gu[i]), bf(f_w_down[i]))

    head_gain = lambda gn: jnp.repeat(gn.astype(F32), D_MODEL // HEAD, axis=0).reshape(1, -1)
    kv = _head_proj(x2d, seq, row1(kv_norm_g), bf(w_kv), head_gain(k_norm_g), Q_WIDTH, 1.0,
                    cos_t, sin_t, seg128)
    kv3 = kv.reshape(bsz, seq, 2 * Q_WIDTH)
    for j in range(n_b):
        q = _head_proj(x2d, seq, row1(b_norm_g[j]), bf(b_w_q[j]), head_gain(b_q_norm_g[j]), Q_WIDTH,
                       HEAD ** -0.5, cos_t, sin_t, seg128)
        q3 = q.reshape(bsz, seq, Q_WIDTH)
        outs, lses = [], []
        for gi, (_, dilation) in enumerate(DILATED_GROUPS):
            o, lse = _attn(q3, kv3, gi, dilation)
            outs.append(o)
            lses.append(lse)
        layer = n_a + j
        x2d = _merge_oproj_ffn(x2d, outs, lses, expand, bf(b_w_o[j]), row1(f_norm_g[layer]),
                               bf(f_w_gu[layer]), bf(f_w_down[layer]))
    return x2d.reshape(bsz, seq, d)
```

```python
import functools

import jax
import jax.numpy as jnp
from jax import lax
from jax.experimental import pallas as pl
from jax.experimental.pallas import tpu as pltpu

F32 = jnp.float32
BF = jnp.bfloat16

D_MODEL = 1024
RMS_EPS = 1e-6
LNX_EPS = 64e-5
HEAD = 64
LANES = 128
N_PAIRS = D_MODEL // LANES
CHUNK = 64
DILATED_GROUPS = ((128, 1), (512, 4), (2048, 16))
N_GROUPS = len(DILATED_GROUPS)
ATT_BLOCK = 128
ROPE_THETA = 10000.0
Q_WIDTH = N_GROUPS * D_MODEL
D_FF = 2816
FF_CHUNK = 1408
NEG_BIG = -1e30
DECAY_SCALE = 0.6065306597126334

ROW_TILE = 256
WKV_TILE = 256
WKV_PAIRS = 4
VMEM_LIMIT = 56 * 1024 * 1024


def _iota(shape, dim):
    return lax.broadcasted_iota(jnp.int32, shape, dim)


def _const_spec(shape):
    nd = len(shape)
    return pl.BlockSpec(shape, lambda *_: (0,) * nd, pipeline_mode=pl.Buffered(1))


def _params(sem):
    return pltpu.CompilerParams(dimension_semantics=sem, vmem_limit_bytes=VMEM_LIMIT)


def _bdot(a, b):
    return jnp.dot(a.astype(BF), b.astype(BF), preferred_element_type=F32)


def _bdot_nt(a, b):
    return lax.dot_general(a.astype(BF), b.astype(BF), (((1,), (1,)), ((), ())),
                           preferred_element_type=F32)


def _bdot_tn(a, b):
    return lax.dot_general(a.astype(BF), b.astype(BF), (((0,), (0,)), ((), ())),
                           preferred_element_type=F32)


def _rms(x, g):
    ms = jnp.mean(x * x, axis=-1, keepdims=True)
    return x * lax.rsqrt(ms + RMS_EPS) * g


def _hi_lo(x):
    hi = x.astype(BF)
    lo = (x - hi.astype(F32)).astype(BF)
    return hi, lo


def _rwkv_proj_body(has_vmix, tiles_per_seq, *refs):
    if has_vmix:
        (x_ref, vf_ref, ng_ref, mu_ref, wr_ref, wk_ref, wv_ref, w0_ref, w1_ref, w2_ref,
         a0_ref, a1_ref, a2_ref, v0_ref, v1_ref, v2_ref, g1_ref, g2_ref,
         r_out, k_out, v_out, ld_out, al_out, g_out, carry_ref) = refs
    else:
        (x_ref, ng_ref, mu_ref, wr_ref, wk_ref, wv_ref, w0_ref, w1_ref, w2_ref,
         a0_ref, a1_ref, a2_ref, g1_ref, g2_ref,
         r_out, k_out, v_out, ld_out, al_out, g_out, carry_ref) = refs
    i = pl.program_id(0)
    tm = x_ref.shape[0]
    h = _rms(x_ref[...], ng_ref[...])

    @pl.when(i % tiles_per_seq == 0)
    def _():
        carry_ref[...] = jnp.zeros_like(carry_ref)

    prev_last = carry_ref[7:8, :]
    hprev = jnp.where(_iota(h.shape, 0) == 0, prev_last, pltpu.roll(h, 1, 0))
    carry_ref[...] = h[tm - 8:tm, :]
    dx = hprev - h
    mu = mu_ref[...]
    mix = lambda j: (h + dx * mu[j:j + 1, :]).astype(BF)
    xr, xw, xk, xv, xa, xg = (mix(j) for j in range(6))

    r_out[...] = _bdot(xr, wr_ref[...]).astype(BF)
    k_out[...] = _bdot(xk, wk_ref[...])
    v = _bdot(xv, wv_ref[...])
    if has_vmix:
        gate = jax.nn.sigmoid(v0_ref[...] + _bdot(_bdot(xv, v1_ref[...]), v2_ref[...]))
        v = v + (vf_ref[...].astype(F32) - v) * gate
    v_out[...] = v.astype(BF)
    z = w0_ref[...] + _bdot(jnp.tanh(_bdot(xw, w1_ref[...])), w2_ref[...])
    ld_out[...] = -DECAY_SCALE * jax.nn.sigmoid(z)
    al_out[...] = jax.nn.sigmoid(a0_ref[...] + _bdot(_bdot(xa, a1_ref[...]), a2_ref[...]))
    g_out[...] = _bdot(jax.nn.sigmoid(_bdot(xg, g1_ref[...])), g2_ref[...]).astype(BF)


def _rwkv_proj(x2d, v_first, seq, ng, mu, wr, wk, wv, w0, w1, w2, a0, a1, a2, vmix, g1, g2):
    m, d = x2d.shape
    tm = ROW_TILE
    has_vmix = vmix is not None
    row = pl.BlockSpec((tm, d), lambda i: (i, 0))
    ins = [x2d] + ([v_first] if has_vmix else []) + [ng, mu, wr, wk, wv, w0, w1, w2, a0, a1, a2]
    ins += (list(vmix) if has_vmix else []) + [g1, g2]
    n_row = 2 if has_vmix else 1
    specs = [row] * n_row + [_const_spec(a.shape) for a in ins[n_row:]]
    out_dt = (BF, F32, BF, F32, F32, BF)
    return pl.pallas_call(
        functools.partial(_rwkv_proj_body, has_vmix, seq // tm),
        grid=(m // tm,),
        in_specs=specs,
        out_specs=[row] * 6,
        out_shape=[jax.ShapeDtypeStruct((m, d), t) for t in out_dt],
        scratch_shapes=[pltpu.VMEM((8, d), F32)],
        compiler_params=_params(("arbitrary",)),
        name="rwkv_proj",
    )(*ins)


def _wkv_masks():
    c = CHUNK
    row = _iota((c, LANES), 0)
    rr = _iota((c, LANES), 1) & (c - 1)
    r128 = _iota((LANES, LANES), 0)
    l128 = _iota((LANES, LANES), 1)
    m = {}
    m["strict"] = row > rr
    m["incl"] = row >= rr
    m["d16s"] = ((row >> 4) == (rr >> 4)) & (row > rr)
    m["off1"] = ((row >> 5) == (rr >> 5)) & (((row >> 4) & 1) == 1) & (((rr >> 4) & 1) == 0)
    m["off2"] = ((row >> 5) == 1) & ((rr >> 5) == 0)
    m["bd16"] = (r128 >> 4) == (l128 >> 4)
    m["bd32"] = (r128 >> 5) == (l128 >> 5)
    m["bd64"] = (r128 >> 6) == (l128 >> 6)
    m["eye"] = r128 == l128
    m["eye16"] = _iota((16, LANES), 0) == (_iota((16, LANES), 1) & 15)
    m["c32"] = (_iota((32, LANES), 0) >> 4) == ((_iota((32, LANES), 1) & 31) >> 4)
    m["c64"] = (_iota((c, LANES), 0) >> 5) == (rr >> 5)
    m["ltri"] = jnp.where(_iota((c, c), 0) >= _iota((c, c), 1), 1.0, 0.0).astype(BF)
    seg = jnp.where(m["bd64"], 1.0, 0.0).astype(BF)
    m["seg"] = jnp.concatenate([seg, seg], axis=0)
    return m


def _seg_sum(x, m):
    hi, lo = _hi_lo(x)
    return jnp.dot(jnp.concatenate([hi, lo], axis=1), m["seg"], preferred_element_type=F32)


def _tile_rows(x, reps, mask):
    return jnp.where(mask, jnp.concatenate([x] * reps, axis=0), 0.0)


def _interleave(gens):
    results = [None] * len(gens)
    live = list(range(len(gens)))
    while live:
        still = []
        for i in live:
            try:
                next(gens[i])
                still.append(i)
            except StopIteration as stop:
                results[i] = stop.value
        live = still
    return results


def _wkv_chunk_gen(r, ld, k_raw, v, al, k_k, k_a, r_k, m):
    c = CHUNK
    kkx = k_raw * k_k
    sq = _seg_sum(kkx * kkx, m)
    h1 = ld.astype(BF)
    r1 = ld - h1.astype(F32)
    h2 = r1.astype(BF)
    h3 = (r1 - h2.astype(F32)).astype(BF)
    cum3 = jnp.dot(m["ltri"], jnp.concatenate([h1, h2, h3], axis=1), preferred_element_type=F32)
    k = k_raw * (1.0 + (al - 1.0) * k_a)
    bonus_s = _seg_sum(r * k * r_k, m)
    yield

    kk = kkx / jnp.maximum(jnp.sqrt(sq), 1e-12)
    a = -kk
    b = kk * al
    cum = cum3[:, :LANES] + cum3[:, LANES:2 * LANES] + cum3[:, 2 * LANES:]
    mid = cum[c // 2 - 1:c // 2, :]
    tot = cum[c - 1:c, :]
    e_in = jnp.exp(cum - mid)
    e_ex = jnp.exp(cum - ld - mid)
    e_inv = jnp.exp(mid - cum)
    e_mid = jnp.exp(mid)
    e_tm = jnp.exp(tot - mid)
    at, rt, bt, kt = a * e_ex, r * e_in, b * e_inv, k * e_inv
    a0, r0, bh, kh = at * e_mid, rt * e_mid, bt * e_tm, kt * e_tm
    stack = lambda x: _tile_rows(x, 2, m["bd64"])
    quad = _bdot_nt(jnp.concatenate([at, rt], axis=0),
                    jnp.concatenate([stack(bt), stack(kt)], axis=0))
    yield

    q_ab = quad[:c, :LANES]
    aak = jnp.where(m["strict"], quad[:c, LANES:], 0.0)
    mrb = jnp.where(m["incl"], quad[c:, :LANES], 0.0)
    mrk = jnp.where(m["incl"], quad[c:, LANES:], 0.0)
    ad = jnp.where(m["d16s"], q_ab, 0.0)
    p1 = ad[0:16] + ad[16:32] + ad[32:48] + ad[48:64]
    full16 = lambda x: _tile_rows(x, 8, m["bd16"])
    p2 = _bdot(p1, full16(p1))
    vst = stack(v)
    x1 = _bdot(aak, vst)
    yield
    t16 = p1 + jnp.where(m["eye16"], 1.0, 0.0)
    p2f = full16(p2)
    t16n = _bdot(t16, p2f)
    p4 = _bdot(p2, p2f)
    yield
    t16 = t16 + t16n
    p4f = full16(p4)
    t16n = _bdot(t16, p4f)
    p8 = _bdot(p4, p4f)
    yield
    t16 = t16 + t16n
    t16n = _bdot(t16, full16(p8))
    yield
    t16 = t16 + t16n
    t32d = _tile_rows(t16, 2, m["c32"])
    a1 = jnp.where(m["off1"], q_ab, 0.0)
    n1 = _bdot(a1[0:32] + a1[32:64], _tile_rows(t32d, 4, m["bd32"]))
    yield
    t32n = _bdot(t32d, _tile_rows(n1, 4, m["bd32"]))
    yield
    t64d = _tile_rows(t32d + t32n, 2, m["c64"])
    n2 = _bdot(jnp.where(m["off2"], q_ab, 0.0), stack(t64d))
    yield
    t64n = _bdot(t64d, stack(n2))
    yield
    tc = t64d + t64n
    wu = _bdot(tc, jnp.concatenate([stack(a0), stack(x1)], axis=1))
    yield
    w, u0 = wu[:, :LANES], wu[:, LANES:]
    qp = r0 + _bdot(mrb, stack(w))
    o0 = _bdot(jnp.concatenate([mrb, mrk], axis=1), jnp.concatenate([stack(u0), vst], axis=0))
    gh = _bdot_tn(jnp.concatenate([bh, kh], axis=0),
                  jnp.concatenate([jnp.concatenate([w, u0], axis=1),
                                   jnp.concatenate([jnp.zeros_like(v), v], axis=1)], axis=0))
    yield
    g = jnp.where(m["bd64"], gh[:, :LANES], 0.0) + jnp.where(m["eye"], jnp.exp(tot), 0.0)
    hh = jnp.where(m["bd64"], gh[:, LANES:], 0.0)
    return qp, o0, g, hh, bonus_s * v


def _wkv_chain_gen(state, steps):
    outs = []
    for qp, o0, g, hh in steps:
        o = _bdot(qp, state) + o0
        state = _bdot(g, state) + hh
        outs.append(o)
        yield
    return state, outs


def _wkv_norm_gen(o, bonus, lnx_g, lnx_b, gate, m):
    mean_s = _seg_sum(o, m)
    yield
    dev = o - mean_s * (1.0 / HEAD)
    var_s = _seg_sum(dev * dev, m)
    yield
    y = dev * lax.rsqrt(var_s * (1.0 / HEAD) + LNX_EPS) * lnx_g + lnx_b + bonus
    return (y * gate).astype(BF)


def _wkv_body(r_ref, k_ref, v_ref, ld_ref, al_ref, g_ref, kk_ref, ka_ref, rk_ref, lg_ref, lb_ref,
              y_ref, state_ref):
    t = pl.program_id(2)

    @pl.when(t == 0)
    def _():
        state_ref[...] = jnp.zeros_like(state_ref)

    m = _wkv_masks()
    n_chunks = r_ref.shape[1] // CHUNK
    n_pairs = r_ref.shape[2] // LANES
    work = [(p, ci) for p in range(n_pairs) for ci in range(n_chunks)]
    rows = lambda ci: pl.ds(ci * CHUNK, CHUNK)
    cols = lambda p: pl.ds(p * LANES, LANES)

    gens = []
    for p, ci in work:
        gens.append(_wkv_chunk_gen(
            r_ref[0, rows(ci), cols(p)].astype(F32), ld_ref[0, rows(ci), cols(p)],
            k_ref[0, rows(ci), cols(p)], v_ref[0, rows(ci), cols(p)].astype(F32),
            al_ref[0, rows(ci), cols(p)], kk_ref[:, cols(p)], ka_ref[:, cols(p)], rk_ref[:, cols(p)], m))
    pre = _interleave(gens)

    chains = _interleave([
        _wkv_chain_gen(state_ref[p], [pre[p * n_chunks + ci][:4] for ci in range(n_chunks)])
        for p in range(n_pairs)])
    for p in range(n_pairs):
        state_ref[p] = chains[p][0]

    ys = _interleave([
        _wkv_norm_gen(chains[p][1][ci], pre[p * n_chunks + ci][4], lg_ref[:, cols(p)], lb_ref[:, cols(p)],
                      g_ref[0, rows(ci), cols(p)].astype(F32), m)
        for p, ci in work])
    for (p, ci), y in zip(work, ys):
        y_ref[0, rows(ci), cols(p)] = y


def _wkv(r, k, v, ld, al, g, k_k, k_a, r_k, lnx_g, lnx_b):
    b, s, d = r.shape
    tc = WKV_TILE
    width = WKV_PAIRS * LANES
    blk = pl.BlockSpec((1, tc, width), lambda bi, hp, t: (bi, t, hp))
    par = pl.BlockSpec((1, width), lambda bi, hp, t: (0, hp))
    return pl.pallas_call(
        _wkv_body,
        grid=(b, d // width, s // tc),
        in_specs=[blk] * 6 + [par] * 5,
        out_specs=blk,
        out_shape=jax.ShapeDtypeStruct((b, s, d), BF),
        scratch_shapes=[pltpu.VMEM((WKV_PAIRS, LANES, LANES), F32)],
        compiler_params=_params(("parallel", "parallel", "arbitrary")),
        name="wkv",
    )(r, k, v, ld, al, g, k_k, k_a, r_k, lnx_g, lnx_b)


def _ffn(x1, fng, wgu_ref, wdown_ref):
    h = _rms(x1, fng).astype(BF)
    acc = x1
    for c in range(D_FF // FF_CHUNK):
        lo = c * FF_CHUNK
        gate = jnp.dot(h, wgu_ref[:, lo:lo + FF_CHUNK], preferred_element_type=F32)
        up = jnp.dot(h, wgu_ref[:, D_FF + lo:D_FF + lo + FF_CHUNK], preferred_element_type=F32)
        act = (gate * jax.nn.sigmoid(gate) * up).astype(BF)
        acc = acc + jnp.dot(act, wdown_ref[lo:lo + FF_CHUNK, :], preferred_element_type=F32)
    return acc


def _oproj_ffn_body(x_ref, y_ref, wo_ref, fng_ref, wgu_ref, wdown_ref, out_ref):
    x1 = x_ref[...] + jnp.dot(y_ref[...], wo_ref[...], preferred_element_type=F32)
    out_ref[...] = _ffn(x1, fng_ref[...], wgu_ref, wdown_ref)


def _oproj_ffn(x2d, y2d, wo, fng, wgu, wdown):
    m, d = x2d.shape
    tm = ROW_TILE
    row = pl.BlockSpec((tm, d), lambda i: (i, 0))
    consts = [wo, fng, wgu, wdown]
    return pl.pallas_call(
        _oproj_ffn_body,
        grid=(m // tm,),
        in_specs=[row, row] + [_const_spec(a.shape) for a in consts],
        out_specs=row,
        out_shape=jax.ShapeDtypeStruct((m, d), F32),
        compiler_params=_params(("parallel",)),
        name="oproj_ffn",
    )(x2d, y2d, *consts)


def _token_order(ref, perm_ref):
    dilation, n = ref.shape[1], ref.shape[2]
    if dilation == 1:
        return ref[0, 0].astype(F32)
    n_blocks = ref.shape[3] // LANES
    for r in range(dilation):
        for cb in range(n_blocks):
            perm_ref[cb, pl.ds(r, n, stride=dilation), :] = ref[0, r, :, pl.ds(cb * LANES, LANES)].astype(F32)
    return jnp.concatenate([perm_ref[cb] for cb in range(n_blocks)], axis=1)


def _merge_oproj_ffn_body(x_ref, o1_ref, o2_ref, o3_ref, l1_ref, l2_ref, l3_ref, ex_ref,
                          wo_ref, fng_ref, wgu_ref, wdown_ref, out_ref, *scratch):
    o_perm, l_perm = scratch[:N_GROUPS], scratch[N_GROUPS:]
    l1, l2, l3 = (_token_order(ref, s) for ref, s in zip((l1_ref, l2_ref, l3_ref), l_perm))
    mx = jnp.maximum(jnp.maximum(l1, l2), l3)
    e1, e2, e3 = jnp.exp(l1 - mx), jnp.exp(l2 - mx), jnp.exp(l3 - mx)
    inv = 1.0 / (e1 + e2 + e3)
    o = jnp.zeros(x_ref.shape, F32)
    for e, o_ref, s in zip((e1, e2, e3), (o1_ref, o2_ref, o3_ref), o_perm):
        hi, lo = _hi_lo(e * inv)
        alpha = jnp.dot(jnp.concatenate([hi, lo], axis=1), ex_ref[...], preferred_element_type=F32)
        o = o + alpha * _token_order(o_ref, s)
    x1 = x_ref[...] + _bdot(o, wo_ref[...])
    out_ref[...] = _ffn(x1, fng_ref[...], wgu_ref, wdown_ref)


def _merge_oproj_ffn(x2d, seq, os_, ls_, expand, wo, fng, wgu, wdown):
    m, d = x2d.shape
    tm = ROW_TILE
    tiles_per_seq = seq // tm
    row = pl.BlockSpec((tm, d), lambda i: (i, 0))
    dil = lambda a: pl.BlockSpec((1, a.shape[1], tm // a.shape[1], a.shape[3]),
                                 lambda i: (i // tiles_per_seq, 0, i % tiles_per_seq, 0))
    consts = [expand, wo, fng, wgu, wdown]
    return pl.pallas_call(
        _merge_oproj_ffn_body,
        grid=(m // tm,),
        in_specs=[row] + [dil(a) for a in os_] + [dil(a) for a in ls_] + [_const_spec(a.shape) for a in consts],
        out_specs=row,
        out_shape=jax.ShapeDtypeStruct((m, d), F32),
        scratch_shapes=([pltpu.VMEM((d // LANES, tm, LANES), F32)] * N_GROUPS
                        + [pltpu.VMEM((1, tm, LANES), F32)] * N_GROUPS),
        compiler_params=_params(("parallel",)),
        name="merge_oproj_ffn",
    )(x2d, *os_, *ls_, *consts)


PROJ_CHUNK = 512


def _head_proj_body(n_rope, scale, x_ref, ng_ref, w_ref, hg_ref, cos_ref, sin_ref, seg_ref, *rest):
    out_refs, perm_ref = rest[:-1], rest[-1]
    tm = x_ref.shape[0]
    h = _rms(x_ref[...], ng_ref[...]).astype(BF)
    cosb, sinb = cos_ref[...], sin_ref[...]
    first_half = (_iota(cosb.shape, 1) & (HEAD - 1)) < HEAD // 2
    for c in range(len(out_refs) * D_MODEL // PROJ_CHUNK):
        lo = c * PROJ_CHUNK
        y = jnp.dot(h, w_ref[:, lo:lo + PROJ_CHUNK], preferred_element_type=F32)
        out_ref = out_refs[lo // D_MODEL]
        dilation = out_ref.shape[1]
        for j in range(PROJ_CHUNK // LANES):
            res = y[:, j * LANES:(j + 1) * LANES]
            col = lo + j * LANES
            if col < n_rope:
                ss = jnp.dot((res * res).astype(BF), seg_ref[...], preferred_element_type=F32)
                yn = res * lax.rsqrt(ss * (1.0 / HEAD) + RMS_EPS) * hg_ref[:, col:col + LANES]
                swapped = jnp.where(first_half, pltpu.roll(yn, LANES - HEAD // 2, 1),
                                    pltpu.roll(yn, HEAD // 2, 1))
                res = (yn * cosb + swapped * sinb) * scale
            ocol = pl.ds(col % D_MODEL, LANES)
            if dilation == 1:
                out_ref[0, 0, :, ocol] = res.astype(BF)
                continue
            perm_ref[j] = res
            for r in range(dilation):
                out_ref[0, r, :, ocol] = perm_ref[j, pl.ds(r, tm // dilation, stride=dilation), :].astype(BF)


def _head_proj(x2d, bsz, seq, ng, w, head_gain, n_rope, scale, cos_t, sin_t, seg):
    m, d = x2d.shape
    n_arrays = w.shape[1] // d
    tm = ROW_TILE
    tiles_per_seq = seq // tm
    row = pl.BlockSpec((tm, d), lambda i: (i, 0))
    tab = pl.BlockSpec((tm, LANES), lambda i: (i % tiles_per_seq, 0))
    dils = [DILATED_GROUPS[a % N_GROUPS][1] for a in range(n_arrays)]
    return pl.pallas_call(
        functools.partial(_head_proj_body, n_rope, scale),
        grid=(m // tm,),
        in_specs=[row, _const_spec(ng.shape), _const_spec(w.shape), _const_spec(head_gain.shape),
                  tab, tab, _const_spec(seg.shape)],
        out_specs=[pl.BlockSpec((1, dl, tm // dl, d),
                                lambda i: (i // tiles_per_seq, 0, i % tiles_per_seq, 0)) for dl in dils],
        out_shape=[jax.ShapeDtypeStruct((bsz, dl, seq // dl, d), BF) for dl in dils],
        scratch_shapes=[pltpu.VMEM((PROJ_CHUNK // LANES, tm, LANES), F32)],
        compiler_params=_params(("parallel",)),
        name="head_proj",
    )(x2d, ng, w, head_gain, cos_t, sin_t, seg)


def _attn_body(q_ref, kp_ref, kc_ref, vp_ref, vc_ref, o_ref, l_ref):
    n = pl.program_id(2)
    blk = ATT_BLOCK
    row = _iota((blk, 2 * blk), 0)
    col = _iota((blk, 2 * blk), 1)
    prev_bias = jnp.where(n > 0, 0.0, NEG_BIG)
    bias = jnp.where(col >= blk, jnp.where(col - blk <= row, 0.0, NEG_BIG),
                     jnp.where(col >= row, prev_bias, NEG_BIG))
    lane = _iota((blk, LANES), 1)
    lse_tile = jnp.zeros((blk, LANES), F32)
    for j in range(q_ref.shape[3] // LANES):
        cols = slice(j * LANES, (j + 1) * LANES)
        q = q_ref[0, 0, :, cols]
        kcat = jnp.concatenate([kp_ref[0, 0, :, cols], kc_ref[0, 0, :, cols]], axis=0)
        vcat = jnp.concatenate([vp_ref[0, 0, :, cols], vc_ref[0, 0, :, cols]], axis=0)
        outs = []
        for hh in range(2):
            in_head = (lane >= hh * HEAD) & (lane < (hh + 1) * HEAD)
            qm = jnp.where(in_head, q, jnp.zeros_like(q))
            s = lax.dot_general(qm, kcat, (((1,), (1,)), ((), ())), preferred_element_type=F32) + bias
            mx = jnp.max(s, axis=-1, keepdims=True)
            p = jnp.exp(s - mx)
            den = jnp.sum(p, axis=-1, keepdims=True)
            pv = jnp.dot(p.astype(BF), vcat, preferred_element_type=F32)
            outs.append(pv / den)
            lse_tile = jnp.where(lane == 2 * j + hh, mx + jnp.log(den), lse_tile)
        o_ref[0, 0, :, cols] = jnp.where(lane < HEAD, outs[0], outs[1]).astype(BF)
    l_ref[0, 0] = lse_tile


def _attn(q, k, v):
    b, dilation, ln, d = q.shape
    blk = ATT_BLOCK
    cur = pl.BlockSpec((1, 1, blk, d), lambda bi, r, n: (bi, r, n, 0))
    prev = pl.BlockSpec((1, 1, blk, d), lambda bi, r, n: (bi, r, jnp.maximum(n - 1, 0), 0))
    return pl.pallas_call(
        _attn_body,
        grid=(b, dilation, ln // blk),
        in_specs=[cur, prev, cur, prev, cur],
        out_specs=[cur, pl.BlockSpec((1, 1, blk, LANES), lambda bi, r, n: (bi, r, n, 0))],
        out_shape=[jax.ShapeDtypeStruct((b, dilation, ln, d), BF),
                   jax.ShapeDtypeStruct((b, dilation, ln, LANES), F32)],
        compiler_params=_params(("parallel", "parallel", "arbitrary")),
        name=f"attn_d{dilation}",
    )(q, k, k, v, v)


def _rope_tables(seq):
    inv = ROPE_THETA ** (-jnp.arange(0, HEAD, 2, dtype=F32) / HEAD)
    ang = jnp.arange(seq, dtype=F32)[:, None] * inv[None, :]
    cos, sin = jnp.cos(ang), jnp.sin(ang)
    cos_t = jnp.tile(cos, (1, LANES // (HEAD // 2)))
    sin_t = jnp.tile(jnp.concatenate([-sin, sin], axis=1), (1, LANES // HEAD))
    return cos_t, sin_t


def kernel(x, a_norm_g, a_mu, a_w_r, a_w_k, a_w_v, a_w0, a_w1, a_w2, a_a0, a_a1, a_a2, a_v0, a_v1, a_v2, a_g1, a_g2, a_k_k, a_k_a, a_r_k, a_lnx_g, a_lnx_b, a_w_o, kv_norm_g, w_kv, k_norm_g, b_norm_g, b_w_q, b_q_norm_g, b_w_o, f_norm_g, f_w_gu, f_w_down):
    bsz, seq, d = x.shape
    assert d == D_MODEL and seq % (ATT_BLOCK * DILATED_GROUPS[-1][1]) == 0
    assert seq % ROW_TILE == 0 and seq % WKV_TILE == 0
    n_a = a_norm_g.shape[0]
    n_b = b_norm_g.shape[0]
    row1 = lambda t: t.reshape(1, -1).astype(F32)
    bf = lambda t: t.astype(BF)

    cos_t, sin_t = _rope_tables(seq)
    seg128 = jnp.where((_iota((LANES, LANES), 0) >> 6) == (_iota((LANES, LANES), 1) >> 6), 1.0, 0.0).astype(BF)
    ex = jnp.where(_iota((LANES, d), 0) == (_iota((LANES, d), 1) >> 6), 1.0, 0.0).astype(BF)
    expand = jnp.concatenate([ex, ex], axis=0)

    x2d = x.reshape(bsz * seq, d)
    v_first = None
    for i in range(n_a):
        vmix = None if i == 0 else (row1(a_v0[i - 1]), bf(a_v1[i - 1]), bf(a_v2[i - 1]))
        mu = jnp.pad(a_mu[i], ((0, 2), (0, 0)))
        r, k, v, ld, al, g = _rwkv_proj(
            x2d, v_first, seq, row1(a_norm_g[i]), mu, bf(a_w_r[i]), bf(a_w_k[i]), bf(a_w_v[i]),
            row1(a_w0[i]), bf(a_w1[i]), bf(a_w2[i]), row1(a_a0[i]), bf(a_a1[i]), bf(a_a2[i]),
            vmix, bf(a_g1[i]), bf(a_g2[i]))
        if i == 0:
            v_first = v
        to3 = lambda t: t.reshape(bsz, seq, d)
        y = _wkv(to3(r), to3(k), to3(v), to3(ld), to3(al), to3(g), row1(a_k_k[i]), row1(a_k_a[i]),
                 row1(a_r_k[i]), row1(a_lnx_g[i]), row1(a_lnx_b[i]))
        x2d = _oproj_ffn(x2d, y.reshape(bsz * seq, d), bf(a_w_o[i]), row1(f_norm_g[i]),
                         bf(f_w_gu[i]), bf(f_w_down[i]))

    head_gain = lambda gn: jnp.repeat(gn.astype(F32), D_MODEL // HEAD, axis=0).reshape(1, -1)
    kvs = _head_proj(x2d, bsz, seq, row1(kv_norm_g), bf(w_kv), head_gain(k_norm_g), Q_WIDTH, 1.0,
                     cos_t, sin_t, seg128)
    for j in range(n_b):
        qs = _head_proj(x2d, bsz, seq, row1(b_norm_g[j]), bf(b_w_q[j]), head_gain(b_q_norm_g[j]),
                        Q_WIDTH, HEAD ** -0.5, cos_t, sin_t, seg128)
        outs, lses = [], []
        for gi in range(N_GROUPS):
            o, lse = _attn(qs[gi], kvs[gi], kvs[N_GROUPS + gi])
            outs.append(o)
            lses.append(lse)
        layer = n_a + j
        x2d = _merge_oproj_ffn(x2d, seq, outs, lses, expand, bf(b_w_o[j]), row1(f_norm_g[layer]),
                               bf(f_w_gu[layer]), bf(f_w_down[layer]))
    return x2d.reshape(bsz, seq, d)
```

```python
import functools

import jax
import jax.numpy as jnp
from jax import lax
from jax.experimental import pallas as pl
from jax.experimental.pallas import tpu as pltpu

F32 = jnp.float32
BF = jnp.bfloat16

D_MODEL = 1024
RMS_EPS = 1e-6
LNX_EPS = 64e-5
HEAD = 64
LANES = 128
N_PAIRS = D_MODEL // LANES
CHUNK = 64
DILATED_GROUPS = ((128, 1), (512, 4), (2048, 16))
N_GROUPS = len(DILATED_GROUPS)
ATT_BLOCK = 128
ROPE_THETA = 10000.0
Q_WIDTH = N_GROUPS * D_MODEL
D_FF = 2816
FF_CHUNK = 1408
NEG_BIG = -1e30
DECAY_SCALE = 0.6065306597126334
LN2 = 0.6931471805599453
LOG2E = 1.4426950408889634

ROW_TILE = 512
WKV_TILE = 256
WKV_PAIRS = 8
VMEM_LIMIT = 56 * 1024 * 1024


def _iota(shape, dim):
    return lax.broadcasted_iota(jnp.int32, shape, dim)


def _const_spec(shape):
    nd = len(shape)
    return pl.BlockSpec(shape, lambda *_: (0,) * nd, pipeline_mode=pl.Buffered(1))


def _params(sem):
    return pltpu.CompilerParams(dimension_semantics=sem, vmem_limit_bytes=VMEM_LIMIT)


def _bdot(a, b):
    return jnp.dot(a.astype(BF), b.astype(BF), preferred_element_type=F32)


def _bdot_nt(a, b):
    return lax.dot_general(a.astype(BF), b.astype(BF), (((1,), (1,)), ((), ())),
                           preferred_element_type=F32)


def _bdot_tn(a, b):
    return lax.dot_general(a.astype(BF), b.astype(BF), (((0,), (0,)), ((), ())),
                           preferred_element_type=F32)


def _rms(x, g):
    ms = jnp.mean(x * x, axis=-1, keepdims=True)
    return x * lax.rsqrt(ms + RMS_EPS) * g


def _hi_lo(x):
    hi = x.astype(BF)
    lo = (x - hi.astype(F32)).astype(BF)
    return hi, lo


def _rwkv_proj_body(has_vmix, tiles_per_seq, *refs):
    if has_vmix:
        (x_ref, vf_ref, ng_ref, mu_ref, wr_ref, wk_ref, wv_ref, w0_ref, w1_ref, w2_ref,
         a0_ref, a1_ref, a2_ref, v0_ref, v1_ref, v2_ref, g1_ref, g2_ref,
         r_out, k_out, v_out, ld_out, al_out, g_out, carry_ref) = refs
    else:
        (x_ref, ng_ref, mu_ref, wr_ref, wk_ref, wv_ref, w0_ref, w1_ref, w2_ref,
         a0_ref, a1_ref, a2_ref, g1_ref, g2_ref,
         r_out, k_out, v_out, ld_out, al_out, g_out, carry_ref) = refs
    i = pl.program_id(0)
    tm = x_ref.shape[0]
    h = _rms(x_ref[...], ng_ref[...])

    @pl.when(i % tiles_per_seq == 0)
    def _():
        carry_ref[...] = jnp.zeros_like(carry_ref)

    prev_last = carry_ref[7:8, :]
    hprev = jnp.where(_iota(h.shape, 0) == 0, prev_last, pltpu.roll(h, 1, 0))
    carry_ref[...] = h[tm - 8:tm, :]
    dx = hprev - h
    mu = mu_ref[...]
    mix = lambda j: (h + dx * mu[j:j + 1, :]).astype(BF)
    xr, xw, xk, xv, xa, xg = (mix(j) for j in range(6))

    def lora(xin, w_in_ref, act, w_out_ref):
        hidden = _bdot(xin, w_in_ref[...])
        yield
        return _bdot(act(hidden), w_out_ref[...])

    def full(xin, w_ref):
        return _bdot(xin, w_ref[...])
        yield

    ident = lambda t: t
    gens = [lora(xw, w1_ref, jnp.tanh, w2_ref), lora(xa, a1_ref, ident, a2_ref),
            lora(xg, g1_ref, jax.nn.sigmoid, g2_ref)]
    if has_vmix:
        gens.append(lora(xv, v1_ref, ident, v2_ref))
    gens += [full(xr, wr_ref), full(xk, wk_ref), full(xv, wv_ref)]
    res = _interleave(gens)
    lw, la, lg = res[:3]
    r, k, v = res[-3:]

    r_out[...] = r.astype(BF)
    k_out[...] = k
    if has_vmix:
        v = v + (vf_ref[...].astype(F32) - v) * jax.nn.sigmoid(v0_ref[...] + res[3])
    v_out[...] = v.astype(BF)
    ld_out[...] = -DECAY_SCALE * jax.nn.sigmoid(w0_ref[...] + lw)
    al_out[...] = jax.nn.sigmoid(a0_ref[...] + la)
    g_out[...] = lg.astype(BF)


def _rwkv_proj(x2d, v_first, seq, ng, mu, wr, wk, wv, w0, w1, w2, a0, a1, a2, vmix, g1, g2):
    m, d = x2d.shape
    tm = ROW_TILE
    has_vmix = vmix is not None
    row = pl.BlockSpec((tm, d), lambda i: (i, 0))
    ins = [x2d] + ([v_first] if has_vmix else []) + [ng, mu, wr, wk, wv, w0, w1, w2, a0, a1, a2]
    ins += (list(vmix) if has_vmix else []) + [g1, g2]
    n_row = 2 if has_vmix else 1
    specs = [row] * n_row + [_const_spec(a.shape) for a in ins[n_row:]]
    out_dt = (BF, F32, BF, F32, F32, BF)
    return pl.pallas_call(
        functools.partial(_rwkv_proj_body, has_vmix, seq // tm),
        grid=(m // tm,),
        in_specs=specs,
        out_specs=[row] * 6,
        out_shape=[jax.ShapeDtypeStruct((m, d), t) for t in out_dt],
        scratch_shapes=[pltpu.VMEM((8, d), F32)],
        compiler_params=_params(("arbitrary",)),
        name="rwkv_proj",
    )(*ins)


def _wkv_masks():
    c = CHUNK
    row = _iota((c, LANES), 0)
    rr = _iota((c, LANES), 1) & (c - 1)
    r128 = _iota((LANES, LANES), 0)
    l128 = _iota((LANES, LANES), 1)
    m = {}
    m["strict"] = row > rr
    m["incl"] = row >= rr
    m["d16s"] = ((row >> 4) == (rr >> 4)) & (row > rr)
    m["off1"] = ((row >> 5) == (rr >> 5)) & (((row >> 4) & 1) == 1) & (((rr >> 4) & 1) == 0)
    m["off2"] = ((row >> 5) == 1) & ((rr >> 5) == 0)
    m["bd16"] = (r128 >> 4) == (l128 >> 4)
    m["bd32"] = (r128 >> 5) == (l128 >> 5)
    m["bd64"] = (r128 >> 6) == (l128 >> 6)
    m["eye"] = r128 == l128
    m["eye16"] = _iota((16, LANES), 0) == (_iota((16, LANES), 1) & 15)
    m["c32"] = (_iota((32, LANES), 0) >> 4) == ((_iota((32, LANES), 1) & 31) >> 4)
    m["c64"] = (_iota((c, LANES), 0) >> 5) == (rr >> 5)
    m["ltri"] = jnp.where(_iota((c, c), 0) >= _iota((c, c), 1), 1.0, 0.0).astype(BF)
    seg = jnp.where(m["bd64"], 1.0, 0.0).astype(BF)
    m["seg"] = jnp.concatenate([seg, seg], axis=0)
    return m


def _seg_sum(x, m):
    hi, lo = _hi_lo(x)
    return jnp.dot(jnp.concatenate([hi, lo], axis=1), m["seg"], preferred_element_type=F32)


def _tile_rows(x, reps, mask):
    return jnp.where(mask, jnp.concatenate([x] * reps, axis=0), 0.0)


def _interleave(gens, batched=None):
    results = [None] * len(gens)
    live = list(range(len(gens)))
    replies = {}
    while live:
        still, asks = [], []
        for i in live:
            try:
                ask = gens[i].send(replies.pop(i, None))
                still.append(i)
                if ask is not None:
                    asks.append((i, ask))
            except StopIteration as stop:
                results[i] = stop.value
        if asks:
            out = batched(jnp.concatenate([a for _, a in asks], axis=0))
            row = 0
            for i, a in asks:
                replies[i] = out[row:row + a.shape[0]]
                row += a.shape[0]
        live = still
    return results


def _pipeline(gens):
    n = len(gens)
    results = [None] * n
    done = [False] * n
    rnd = 0
    while not all(done):
        for i in reversed(range(min(rnd + 1, n))):
            if not done[i]:
                try:
                    next(gens[i])
                except StopIteration as stop:
                    results[i] = stop.value
                    done[i] = True
        rnd += 1
    return results


def _wkv_chunk_gen(r, ld, k_raw, v, al, k_k, k_a, r_k, m):
    c = CHUNK
    kkx = k_raw * k_k
    k = k_raw * (1.0 + (al - 1.0) * k_a)
    h1 = ld.astype(BF)
    r1 = ld - h1.astype(F32)
    h2 = r1.astype(BF)
    h3 = (r1 - h2.astype(F32)).astype(BF)
    cum3 = jnp.dot(m["ltri"], jnp.concatenate([h1, h2, h3], axis=1), preferred_element_type=F32)
    sums = yield jnp.concatenate([kkx * kkx, r * k * r_k], axis=0)
    sq, bonus_s = sums[:c], sums[c:]

    kk = kkx / jnp.maximum(jnp.sqrt(sq), 1e-12)
    a = -kk
    b = kk * al
    cum = cum3[:, :LANES] + cum3[:, LANES:2 * LANES] + cum3[:, 2 * LANES:]
    mid = cum[c // 2 - 1:c // 2, :]
    tot = cum[c - 1:c, :]
    e_in = jnp.exp(cum - mid)
    e_ex = jnp.exp(cum - ld - mid)
    e_inv = jnp.exp(mid - cum)
    e_mid = jnp.exp(mid)
    e_tm = jnp.exp(tot - mid)
    at, rt, bt, kt = a * e_ex, r * e_in, b * e_inv, k * e_inv
    a0, r0, bh, kh = at * e_mid, rt * e_mid, bt * e_tm, kt * e_tm
    stack = lambda x: _tile_rows(x, 2, m["bd64"])
    quad = _bdot_nt(jnp.concatenate([at, rt], axis=0),
                    jnp.concatenate([stack(bt), stack(kt)], axis=0))
    yield

    q_ab = quad[:c, :LANES]
    aak = jnp.where(m["strict"], quad[:c, LANES:], 0.0)
    mrb = jnp.where(m["incl"], quad[c:, :LANES], 0.0)
    mrk = jnp.where(m["incl"], quad[c:, LANES:], 0.0)
    ad = jnp.where(m["d16s"], q_ab, 0.0)
    p1 = ad[0:16] + ad[16:32] + ad[32:48] + ad[48:64]
    full16 = lambda x: _tile_rows(x, 8, m["bd16"])
    p = _bdot(p1, full16(p1))
    vst = stack(v)
    x1 = _bdot(aak, vst)
    yield
    t16 = p1 + jnp.where(m["eye16"], 1.0, 0.0)
    for _ in range(2):
        both = _bdot(jnp.concatenate([t16, p], axis=0), full16(p))
        yield
        t16 = t16 + both[:16]
        p = both[16:]
    t16n = _bdot(t16, full16(p))
    yield
    t16 = t16 + t16n
    t32d = _tile_rows(t16, 2, m["c32"])
    a1 = jnp.where(m["off1"], q_ab, 0.0)
    n1 = _bdot(a1[0:32] + a1[32:64], _tile_rows(t32d, 4, m["bd32"]))
    yield
    t32n = _bdot(t32d, _tile_rows(n1, 4, m["bd32"]))
    yield
    t64d = _tile_rows(t32d + t32n, 2, m["c64"])
    n2 = _bdot(jnp.where(m["off2"], q_ab, 0.0), stack(t64d))
    yield
    t64n = _bdot(t64d, stack(n2))
    yield
    tc = t64d + t64n
    wu = _bdot(tc, jnp.concatenate([stack(a0), stack(x1)], axis=1))
    yield
    w, u0 = wu[:, :LANES], wu[:, LANES:]
    qp = r0 + _bdot(mrb, stack(w))
    o0 = _bdot(jnp.concatenate([mrb, mrk], axis=1), jnp.concatenate([stack(u0), vst], axis=0))
    gh = _bdot_tn(jnp.concatenate([bh, kh], axis=0),
                  jnp.concatenate([jnp.concatenate([w, u0], axis=1),
                                   jnp.concatenate([jnp.zeros_like(v), v], axis=1)], axis=0))
    yield
    g = jnp.where(m["bd64"], gh[:, :LANES], 0.0) + jnp.where(m["eye"], jnp.exp(tot), 0.0)
    hh = jnp.where(m["bd64"], gh[:, LANES:], 0.0)
    return qp, o0, g, hh, bonus_s * v


def _wkv_chain_gen(state, steps):
    outs = []
    for qp, o0, g, hh in steps:
        both = _bdot(jnp.concatenate([qp, g], axis=0), state)
        outs.append(both[:CHUNK] + o0)
        state = both[CHUNK:] + hh
        yield
    return state, outs


def _wkv_norm_gen(o, bonus, lnx_g, lnx_b, gate, m):
    mean_s = yield o
    dev = o - mean_s * (1.0 / HEAD)
    var_s = yield dev * dev
    y = dev * lax.rsqrt(var_s * (1.0 / HEAD) + LNX_EPS) * lnx_g + lnx_b + bonus
    return (y * gate).astype(BF)


def _wkv_body(r_ref, k_ref, v_ref, ld_ref, al_ref, g_ref, kk_ref, ka_ref, rk_ref, lg_ref, lb_ref,
              y_ref, state_ref):
    t = pl.program_id(2)

    @pl.when(t == 0)
    def _():
        state_ref[...] = jnp.zeros_like(state_ref)

    m = _wkv_masks()
    n_chunks = r_ref.shape[1] // CHUNK
    n_pairs = r_ref.shape[2] // LANES
    work = [(p, ci) for p in range(n_pairs) for ci in range(n_chunks)]
    rows = lambda ci: pl.ds(ci * CHUNK, CHUNK)
    cols = lambda p: pl.ds(p * LANES, LANES)

    gens = []
    for p, ci in work:
        gens.append(_wkv_chunk_gen(
            r_ref[0, rows(ci), cols(p)].astype(F32), ld_ref[0, rows(ci), cols(p)],
            k_ref[0, rows(ci), cols(p)], v_ref[0, rows(ci), cols(p)].astype(F32),
            al_ref[0, rows(ci), cols(p)], kk_ref[:, cols(p)], ka_ref[:, cols(p)], rk_ref[:, cols(p)], m))
    seg_sums = functools.partial(_seg_sum, m=m)
    pre = _interleave(gens, seg_sums)

    chains = _interleave([
        _wkv_chain_gen(state_ref[p], [pre[p * n_chunks + ci][:4] for ci in range(n_chunks)])
        for p in range(n_pairs)])
    for p in range(n_pairs):
        state_ref[p] = chains[p][0]

    ys = _interleave([
        _wkv_norm_gen(chains[p][1][ci], pre[p * n_chunks + ci][4], lg_ref[:, cols(p)], lb_ref[:, cols(p)],
                      g_ref[0, rows(ci), cols(p)].astype(F32), m)
        for p, ci in work], seg_sums)
    for (p, ci), y in zip(work, ys):
        y_ref[0, rows(ci), cols(p)] = y


def _wkv(r, k, v, ld, al, g, k_k, k_a, r_k, lnx_g, lnx_b):
    b, s, d = r.shape
    tc = WKV_TILE
    width = WKV_PAIRS * LANES
    blk = pl.BlockSpec((1, tc, width), lambda bi, hp, t: (bi, t, hp))
    par = pl.BlockSpec((1, width), lambda bi, hp, t: (0, hp))
    return pl.pallas_call(
        _wkv_body,
        grid=(b, d // width, s // tc),
        in_specs=[blk] * 6 + [par] * 5,
        out_specs=blk,
        out_shape=jax.ShapeDtypeStruct((b, s, d), BF),
        scratch_shapes=[pltpu.VMEM((WKV_PAIRS, LANES, LANES), F32)],
        compiler_params=_params(("parallel", "parallel", "arbitrary")),
        name="wkv",
    )(r, k, v, ld, al, g, k_k, k_a, r_k, lnx_g, lnx_b)


def _ffn(x1, fng, wgu_ref, wdown_ref):
    h = _rms(x1, fng).astype(BF)

    def ff_chunk(lo):
        gate = jnp.dot(h, wgu_ref[:, lo:lo + FF_CHUNK], preferred_element_type=F32)
        up = jnp.dot(h, wgu_ref[:, D_FF + lo:D_FF + lo + FF_CHUNK], preferred_element_type=F32)
        yield
        act = (gate * jax.nn.sigmoid(gate) * up).astype(BF)
        return jnp.dot(act, wdown_ref[lo:lo + FF_CHUNK, :], preferred_element_type=F32)

    acc = x1
    for part in _pipeline([ff_chunk(c * FF_CHUNK) for c in range(D_FF // FF_CHUNK)]):
        acc = acc + part
    return acc


def _oproj_ffn_body(x_ref, y_ref, wo_ref, fng_ref, wgu_ref, wdown_ref, out_ref):
    x1 = x_ref[...] + jnp.dot(y_ref[...], wo_ref[...], preferred_element_type=F32)
    out_ref[...] = _ffn(x1, fng_ref[...], wgu_ref, wdown_ref)


def _oproj_ffn(x2d, y2d, wo, fng, wgu, wdown):
    m, d = x2d.shape
    tm = ROW_TILE
    row = pl.BlockSpec((tm, d), lambda i: (i, 0))
    consts = [wo, fng, wgu, wdown]
    return pl.pallas_call(
        _oproj_ffn_body,
        grid=(m // tm,),
        in_specs=[row, row] + [_const_spec(a.shape) for a in consts],
        out_specs=row,
        out_shape=jax.ShapeDtypeStruct((m, d), F32),
        compiler_params=_params(("parallel",)),
        name="oproj_ffn",
    )(x2d, y2d, *consts)


def _token_order(ref, perm_ref):
    dilation, n = ref.shape[1], ref.shape[2]
    if dilation == 1:
        return ref[0, 0].astype(F32)
    n_blocks = ref.shape[3] // LANES
    for r in range(dilation):
        for cb in range(n_blocks):
            perm_ref[cb, pl.ds(r, n, stride=dilation), :] = ref[0, r, :, pl.ds(cb * LANES, LANES)].astype(F32)
    return jnp.concatenate([perm_ref[cb] for cb in range(n_blocks)], axis=1)


def _merge_oproj_ffn_body(x_ref, o1_ref, o2_ref, o3_ref, l1_ref, l2_ref, l3_ref, ex_ref,
                          wo_ref, fng_ref, wgu_ref, wdown_ref, out_ref, *scratch):
    o_perm, l_perm = scratch[:N_GROUPS], scratch[N_GROUPS:]
    l1, l2, l3 = (_token_order(ref, s) for ref, s in zip((l1_ref, l2_ref, l3_ref), l_perm))
    mx = jnp.maximum(jnp.maximum(l1, l2), l3)
    e1, e2, e3 = jnp.exp(l1 - mx), jnp.exp(l2 - mx), jnp.exp(l3 - mx)
    inv = 1.0 / (e1 + e2 + e3)
    o = jnp.zeros(x_ref.shape, F32)
    for e, o_ref, s in zip((e1, e2, e3), (o1_ref, o2_ref, o3_ref), o_perm):
        hi, lo = _hi_lo(e * inv)
        alpha = jnp.dot(jnp.concatenate([hi, lo], axis=1), ex_ref[...], preferred_element_type=F32)
        o = o + alpha * _token_order(o_ref, s)
    x1 = x_ref[...] + _bdot(o, wo_ref[...])
    out_ref[...] = _ffn(x1, fng_ref[...], wgu_ref, wdown_ref)


def _merge_oproj_ffn(x2d, seq, os_, ls_, expand, wo, fng, wgu, wdown):
    m, d = x2d.shape
    tm = ROW_TILE
    tiles_per_seq = seq // tm
    row = pl.BlockSpec((tm, d), lambda i: (i, 0))
    dil = lambda a: pl.BlockSpec((1, a.shape[1], tm // a.shape[1], a.shape[3]),
                                 lambda i: (i // tiles_per_seq, 0, i % tiles_per_seq, 0))
    consts = [expand, wo, fng, wgu, wdown]
    return pl.pallas_call(
        _merge_oproj_ffn_body,
        grid=(m // tm,),
        in_specs=[row] + [dil(a) for a in os_] + [dil(a) for a in ls_] + [_const_spec(a.shape) for a in consts],
        out_specs=row,
        out_shape=jax.ShapeDtypeStruct((m, d), F32),
        scratch_shapes=([pltpu.VMEM((d // LANES, tm, LANES), F32)] * N_GROUPS
                        + [pltpu.VMEM((1, tm, LANES), F32)] * N_GROUPS),
        compiler_params=_params(("parallel",)),
        name="merge_oproj_ffn",
    )(x2d, *os_, *ls_, *consts)


PROJ_CHUNK = 512
SEG_WIDTH = 256


def _head_proj_body(n_rope, x_ref, ng_ref, w_ref, cos_ref, sin_ref, seg_ref, *rest):
    out_refs, perm_ref = rest[:-1], rest[-1]
    tm = x_ref.shape[0]
    h = _rms(x_ref[...], ng_ref[...]).astype(BF)

    def column_chunk(lo):
        y = jnp.dot(h, w_ref[:, lo:lo + PROJ_CHUNK], preferred_element_type=F32)
        yield
        out_ref = out_refs[lo // D_MODEL]
        dilation = out_ref.shape[1]
        if lo < n_rope:
            group = lo // D_MODEL
            ss = jnp.concatenate(
                [jnp.dot((ys * ys).astype(BF), seg_ref[...], preferred_element_type=F32)
                 for ys in (y[:, :SEG_WIDTH], y[:, SEG_WIDTH:])], axis=1)
            yield
            y = y * lax.rsqrt(ss * (1.0 / HEAD) + RMS_EPS)
        for j in range(PROJ_CHUNK // LANES):
            res = y[:, j * LANES:(j + 1) * LANES]
            col = lo + j * LANES
            if lo < n_rope:
                res = res * cos_ref[group] + pltpu.roll(res, LANES // 2, 1) * sin_ref[group]
            ocol = pl.ds(col % D_MODEL, LANES)
            if dilation == 1:
                out_ref[0, 0, :, ocol] = res.astype(BF)
                continue
            perm_ref[j] = res
            for r in range(dilation):
                out_ref[0, r, :, ocol] = perm_ref[j, pl.ds(r, tm // dilation, stride=dilation), :].astype(BF)

    _pipeline([column_chunk(c * PROJ_CHUNK) for c in range(len(out_refs) * D_MODEL // PROJ_CHUNK)])


def _head_proj(x2d, bsz, seq, ng, w, n_rope, cos_t, sin_t, seg):
    m, d = x2d.shape
    n_arrays = w.shape[1] // d
    tm = ROW_TILE
    tiles_per_seq = seq // tm
    row = pl.BlockSpec((tm, d), lambda i: (i, 0))
    tab = pl.BlockSpec((N_GROUPS, tm, LANES), lambda i: (0, i % tiles_per_seq, 0))
    dils = [DILATED_GROUPS[a % N_GROUPS][1] for a in range(n_arrays)]
    return pl.pallas_call(
        functools.partial(_head_proj_body, n_rope),
        grid=(m // tm,),
        in_specs=[row, _const_spec(ng.shape), _const_spec(w.shape), tab, tab, _const_spec(seg.shape)],
        out_specs=[pl.BlockSpec((1, dl, tm // dl, d),
                                lambda i: (i // tiles_per_seq, 0, i % tiles_per_seq, 0)) for dl in dils],
        out_shape=[jax.ShapeDtypeStruct((bsz, dl, seq // dl, d), BF) for dl in dils],
        scratch_shapes=[pltpu.VMEM((PROJ_CHUNK // LANES, tm, LANES), F32)],
        compiler_params=_params(("parallel",)),
        name="head_proj",
    )(x2d, ng, w, cos_t, sin_t, seg)


def _attn_body(q_ref, kp_ref, kc_ref, vp_ref, vc_ref, o_ref, l_ref):
    n = pl.program_id(2)
    blk = ATT_BLOCK
    row = _iota((blk, 2 * blk), 0)
    col = _iota((blk, 2 * blk), 1)
    prev_bias = jnp.where(n > 0, 0.0, NEG_BIG)
    bias = jnp.where(col >= blk, jnp.where(col - blk <= row, 0.0, NEG_BIG),
                     jnp.where(col >= row, prev_bias, NEG_BIG))
    lane = _iota((blk, LANES), 1)
    left = lane < HEAD
    q_first = (lane & (HEAD // 2)) == 0

    def head_pair(j):
        cols = slice(j * LANES, (j + 1) * LANES)
        q = q_ref[0, 0, :, cols]
        kcat = jnp.concatenate([kp_ref[0, 0, :, cols], kc_ref[0, 0, :, cols]], axis=0)
        scores = [lax.dot_general(jnp.where(mask, q, jnp.zeros_like(q)), kcat, (((1,), (1,)), ((), ())),
                                  preferred_element_type=F32) for mask in (q_first, ~q_first)]
        yield
        vcat = jnp.concatenate([vp_ref[0, 0, :, cols], vc_ref[0, 0, :, cols]], axis=0)
        scores = [s + bias for s in scores]
        mxs = [jnp.max(s, axis=-1, keepdims=True) for s in scores]
        yield
        pvs, dens = [], []
        for s, mx in zip(scores, mxs):
            p = jnp.exp2(s - mx)
            dens.append(jnp.sum(p, axis=-1, keepdims=True))
            pvs.append(jnp.dot(p.astype(BF), vcat, preferred_element_type=F32))
        yield
        pair = lambda c: jnp.where(left, c[0], c[1])
        den = pair(dens)
        o_ref[0, 0, :, cols] = (pair(pvs) / den).astype(BF)
        return (pair(mxs) + jnp.log2(den)) * LN2

    lses = _pipeline([head_pair(j) for j in range(q_ref.shape[3] // LANES)])
    lse_tile = jnp.zeros((blk, LANES), F32)
    for j, lse in enumerate(lses):
        lse_tile = jnp.where((lane & (HEAD - 1)) == j, lse, lse_tile)
    l_ref[0, 0] = lse_tile


def _attn(q, k, v):
    b, dilation, ln, d = q.shape
    blk = ATT_BLOCK
    cur = pl.BlockSpec((1, 1, blk, d), lambda bi, r, n: (bi, r, n, 0))
    prev = pl.BlockSpec((1, 1, blk, d), lambda bi, r, n: (bi, r, jnp.maximum(n - 1, 0), 0))
    return pl.pallas_call(
        _attn_body,
        grid=(b, dilation, ln // blk),
        in_specs=[cur, prev, cur, prev, cur],
        out_specs=[cur, pl.BlockSpec((1, 1, blk, LANES), lambda bi, r, n: (bi, r, n, 0))],
        out_shape=[jax.ShapeDtypeStruct((b, dilation, ln, d), BF),
                   jax.ShapeDtypeStruct((b, dilation, ln, LANES), F32)],
        compiler_params=_params(("parallel", "parallel", "arbitrary")),
        name=f"attn_d{dilation}",
    )(q, k, k, v, v)


def _rotary_columns(w):
    k, n = w.shape
    half = HEAD // 2
    return w.reshape(k, n // LANES, 2, 2, half).transpose(0, 1, 3, 2, 4).reshape(k, n)


def _rope_tables(seq, gains, scale):
    half = HEAD // 2
    inv = ROPE_THETA ** (-jnp.arange(0, HEAD, 2, dtype=F32) / HEAD)
    ang = jnp.arange(seq, dtype=F32)[:, None] * inv[None, :]
    cos, sin = jnp.cos(ang), jnp.sin(ang)
    cos_t = jnp.concatenate([cos, cos, cos, cos], axis=1)
    sin_t = jnp.concatenate([-sin, -sin, sin, sin], axis=1)
    gains = gains.astype(F32) * scale
    lo, hi = gains[:, :half], gains[:, half:]
    own = jnp.concatenate([lo, lo, hi, hi], axis=1)
    partner = jnp.concatenate([hi, hi, lo, lo], axis=1)
    return cos_t[None] * own[:, None, :], sin_t[None] * partner[:, None, :]


def kernel(x, a_norm_g, a_mu, a_w_r, a_w_k, a_w_v, a_w0, a_w1, a_w2, a_a0, a_a1, a_a2, a_v0, a_v1, a_v2, a_g1, a_g2, a_k_k, a_k_a, a_r_k, a_lnx_g, a_lnx_b, a_w_o, kv_norm_g, w_kv, k_norm_g, b_norm_g, b_w_q, b_q_norm_g, b_w_o, f_norm_g, f_w_gu, f_w_down):
    bsz, seq, d = x.shape
    assert d == D_MODEL and seq % (ATT_BLOCK * DILATED_GROUPS[-1][1]) == 0
    assert seq % ROW_TILE == 0 and seq % WKV_TILE == 0
    n_a = a_norm_g.shape[0]
    n_b = b_norm_g.shape[0]
    row1 = lambda t: t.reshape(1, -1).astype(F32)
    bf = lambda t: t.astype(BF)

    seg_r, seg_c = _iota((SEG_WIDTH, SEG_WIDTH), 0), _iota((SEG_WIDTH, SEG_WIDTH), 1)
    seg = jnp.where(((seg_r >> 7) == (seg_c >> 7)) & (((seg_r ^ seg_c) & (HEAD // 2)) == 0),
                    1.0, 0.0).astype(BF)
    ex_row, ex_col = _iota((LANES, d), 0), _iota((LANES, d), 1)
    ex_head = 2 * (ex_row & (HEAD - 1)) + (ex_row >> 6)
    ex = jnp.where(((ex_row & (HEAD - 1)) < N_PAIRS) & (ex_head == (ex_col >> 6)), 1.0, 0.0).astype(BF)
    expand = jnp.concatenate([ex, ex], axis=0)

    x2d = x.reshape(bsz * seq, d)
    v_first = None
    for i in range(n_a):
        vmix = None if i == 0 else (row1(a_v0[i - 1]), bf(a_v1[i - 1]), bf(a_v2[i - 1]))
        mu = jnp.pad(a_mu[i], ((0, 2), (0, 0)))
        r, k, v, ld, al, g = _rwkv_proj(
            x2d, v_first, seq, row1(a_norm_g[i]), mu, bf(a_w_r[i]), bf(a_w_k[i]), bf(a_w_v[i]),
            row1(a_w0[i]), bf(a_w1[i]), bf(a_w2[i]), row1(a_a0[i]), bf(a_a1[i]), bf(a_a2[i]),
            vmix, bf(a_g1[i]), bf(a_g2[i]))
        if i == 0:
            v_first = v
        to3 = lambda t: t.reshape(bsz, seq, d)
        y = _wkv(to3(r), to3(k), to3(v), to3(ld), to3(al), to3(g), row1(a_k_k[i]), row1(a_k_a[i]),
                 row1(a_r_k[i]), row1(a_lnx_g[i]), row1(a_lnx_b[i]))
        x2d = _oproj_ffn(x2d, y.reshape(bsz * seq, d), bf(a_w_o[i]), row1(f_norm_g[i]),
                         bf(f_w_gu[i]), bf(f_w_down[i]))

    w_kv_r = jnp.concatenate([_rotary_columns(bf(w_kv[:, :Q_WIDTH])), bf(w_kv[:, Q_WIDTH:])], axis=1)
    kvs = _head_proj(x2d, bsz, seq, row1(kv_norm_g), w_kv_r, Q_WIDTH,
                     *_rope_tables(seq, k_norm_g, 1.0), seg)
    for j in range(n_b):
        qs = _head_proj(x2d, bsz, seq, row1(b_norm_g[j]), _rotary_columns(bf(b_w_q[j])), Q_WIDTH,
                        *_rope_tables(seq, b_q_norm_g[j], HEAD ** -0.5 * LOG2E), seg)
        outs, lses = [], []
        for gi in range(N_GROUPS):
            o, lse = _attn(qs[gi], kvs[gi], kvs[N_GROUPS + gi])
            outs.append(o)
            lses.append(lse)
        layer = n_a + j
        x2d = _merge_oproj_ffn(x2d, seq, outs, lses, expand, bf(b_w_o[j]), row1(f_norm_g[layer]),
                               bf(f_w_gu[layer]), bf(f_w_down[layer]))
    return x2d.reshape(bsz, seq, d)
```

```python
import functools

import jax
import jax.numpy as jnp
from jax import lax
from jax.experimental import pallas as pl
from jax.experimental.pallas import tpu as pltpu

F32 = jnp.float32
BF = jnp.bfloat16

D_MODEL = 1024
RMS_EPS = 1e-6
LNX_EPS = 64e-5
HEAD = 64
LANES = 128
N_PAIRS = D_MODEL // LANES
CHUNK = 64
DILATED_GROUPS = ((128, 1), (512, 4), (2048, 16))
N_GROUPS = len(DILATED_GROUPS)
ATT_BLOCK = 128
ATT_STEP = 4
ROPE_THETA = 10000.0
Q_WIDTH = N_GROUPS * D_MODEL
D_FF = 2816
MXU_WIDTH = 256
FF_SPLITS = (0, 6 * MXU_WIDTH, D_FF)
NEG_BIG = -1e30
DECAY_SCALE = 0.6065306597126334
LN2 = 0.6931471805599453
LOG2E = 1.4426950408889634

ROW_TILE = 512
WKV_TILE = 256
WKV_PAIRS = 8
VMEM_LIMIT = 56 * 1024 * 1024


def _iota(shape, dim):
    return lax.broadcasted_iota(jnp.int32, shape, dim)


def _const_spec(shape):
    nd = len(shape)
    return pl.BlockSpec(shape, lambda *_: (0,) * nd, pipeline_mode=pl.Buffered(1))


def _params(sem):
    return pltpu.CompilerParams(dimension_semantics=sem, vmem_limit_bytes=VMEM_LIMIT)


def _bdot(a, b):
    return jnp.dot(a.astype(BF), b.astype(BF), preferred_element_type=F32)


def _bdot_nt(a, b):
    return lax.dot_general(a.astype(BF), b.astype(BF), (((1,), (1,)), ((), ())),
                           preferred_element_type=F32)


def _bdot_tn(a, b):
    return lax.dot_general(a.astype(BF), b.astype(BF), (((0,), (0,)), ((), ())),
                           preferred_element_type=F32)


def _rms(x, g):
    ms = jnp.mean(x * x, axis=-1, keepdims=True)
    return x * lax.rsqrt(ms + RMS_EPS) * g


def _hi_lo(x):
    hi = x.astype(BF)
    lo = (x - hi.astype(F32)).astype(BF)
    return hi, lo


def _rwkv_proj_body(has_vmix, tiles_per_seq, *refs):
    if has_vmix:
        (x_ref, vf_ref, ng_ref, mu_ref, wr_ref, wk_ref, wv_ref, w0_ref, w1_ref, w2_ref,
         a0_ref, a1_ref, a2_ref, v0_ref, v1_ref, v2_ref, g1_ref, g2_ref,
         r_out, k_out, v_out, ld_out, al_out, g_out, carry_ref) = refs
    else:
        (x_ref, ng_ref, mu_ref, wr_ref, wk_ref, wv_ref, w0_ref, w1_ref, w2_ref,
         a0_ref, a1_ref, a2_ref, g1_ref, g2_ref,
         r_out, k_out, v_out, ld_out, al_out, g_out, carry_ref) = refs
    i = pl.program_id(0)
    tm = x_ref.shape[0]
    h = _rms(x_ref[...], ng_ref[...])

    @pl.when(i % tiles_per_seq == 0)
    def _():
        carry_ref[...] = jnp.zeros_like(carry_ref)

    prev_last = carry_ref[7:8, :]
    hprev = jnp.where(_iota(h.shape, 0) == 0, prev_last, pltpu.roll(h, 1, 0))
    carry_ref[...] = h[tm - 8:tm, :]
    dx = hprev - h
    mu = mu_ref[...]
    mix = lambda j: (h + dx * mu[j:j + 1, :]).astype(BF)
    xr, xw, xk, xv, xa, xg = (mix(j) for j in range(6))

    def lora(xin, w_in_ref, act, w_out_ref):
        hidden = _bdot(xin, w_in_ref[...])
        yield
        return _bdot(act(hidden), w_out_ref[...])

    def full(xin, w_ref):
        return _bdot(xin, w_ref[...])
        yield

    ident = lambda t: t
    gens = [lora(xw, w1_ref, jnp.tanh, w2_ref), lora(xa, a1_ref, ident, a2_ref),
            lora(xg, g1_ref, jax.nn.sigmoid, g2_ref)]
    if has_vmix:
        gens.append(lora(xv, v1_ref, ident, v2_ref))
    gens += [full(xr, wr_ref), full(xk, wk_ref), full(xv, wv_ref)]
    res = _interleave(gens)
    lw, la, lg = res[:3]
    r, k, v = res[-3:]

    r_out[...] = r.astype(BF)
    k_out[...] = k
    if has_vmix:
        v = v + (vf_ref[...].astype(F32) - v) * jax.nn.sigmoid(v0_ref[...] + res[3])
    v_out[...] = v.astype(BF)
    ld_out[...] = -DECAY_SCALE * jax.nn.sigmoid(w0_ref[...] + lw)
    al_out[...] = jax.nn.sigmoid(a0_ref[...] + la)
    g_out[...] = lg.astype(BF)


def _rwkv_proj(x2d, v_first, seq, ng, mu, wr, wk, wv, w0, w1, w2, a0, a1, a2, vmix, g1, g2):
    m, d = x2d.shape
    tm = ROW_TILE
    has_vmix = vmix is not None
    row = pl.BlockSpec((tm, d), lambda i: (i, 0))
    ins = [x2d] + ([v_first] if has_vmix else []) + [ng, mu, wr, wk, wv, w0, w1, w2, a0, a1, a2]
    ins += (list(vmix) if has_vmix else []) + [g1, g2]
    n_row = 2 if has_vmix else 1
    specs = [row] * n_row + [_const_spec(a.shape) for a in ins[n_row:]]
    out_dt = (BF, F32, BF, F32, F32, BF)
    return pl.pallas_call(
        functools.partial(_rwkv_proj_body, has_vmix, seq // tm),
        grid=(m // tm,),
        in_specs=specs,
        out_specs=[row] * 6,
        out_shape=[jax.ShapeDtypeStruct((m, d), t) for t in out_dt],
        scratch_shapes=[pltpu.VMEM((8, d), F32)],
        compiler_params=_params(("arbitrary",)),
        name="rwkv_proj",
    )(*ins)


def _wkv_masks():
    c = CHUNK
    row = _iota((c, LANES), 0)
    rr = _iota((c, LANES), 1) & (c - 1)
    r128 = _iota((LANES, LANES), 0)
    l128 = _iota((LANES, LANES), 1)
    m = {}
    m["strict"] = row > rr
    m["incl"] = row >= rr
    m["d16s"] = ((row >> 4) == (rr >> 4)) & (row > rr)
    m["off1"] = ((row >> 5) == (rr >> 5)) & (((row >> 4) & 1) == 1) & (((rr >> 4) & 1) == 0)
    m["off2"] = ((row >> 5) == 1) & ((rr >> 5) == 0)
    m["bd16"] = (r128 >> 4) == (l128 >> 4)
    m["bd32"] = (r128 >> 5) == (l128 >> 5)
    m["bd64"] = (r128 >> 6) == (l128 >> 6)
    m["eye"] = r128 == l128
    m["eye16"] = _iota((16, LANES), 0) == (_iota((16, LANES), 1) & 15)
    m["c32"] = (_iota((32, LANES), 0) >> 4) == ((_iota((32, LANES), 1) & 31) >> 4)
    m["c64"] = (_iota((c, LANES), 0) >> 5) == (rr >> 5)
    m["ltri"] = jnp.where(_iota((c, c), 0) >= _iota((c, c), 1), 1.0, 0.0).astype(BF)
    seg = jnp.where(m["bd64"], 1.0, 0.0).astype(BF)
    m["seg"] = jnp.concatenate([seg, seg], axis=0)
    return m


def _seg_sum(x, m):
    hi, lo = _hi_lo(x)
    return jnp.dot(jnp.concatenate([hi, lo], axis=1), m["seg"], preferred_element_type=F32)


def _tile_rows(x, reps, mask):
    return jnp.where(mask, jnp.concatenate([x] * reps, axis=0), 0.0)


def _interleave(gens, batched=None):
    results = [None] * len(gens)
    live = list(range(len(gens)))
    replies = {}
    while live:
        still, asks = [], []
        for i in live:
            try:
                ask = gens[i].send(replies.pop(i, None))
                still.append(i)
                if ask is not None:
                    asks.append((i, ask))
            except StopIteration as stop:
                results[i] = stop.value
        if asks:
            out = batched(jnp.concatenate([a for _, a in asks], axis=0))
            row = 0
            for i, a in asks:
                replies[i] = out[row:row + a.shape[0]]
                row += a.shape[0]
        live = still
    return results


def _pipeline(gens):
    n = len(gens)
    results = [None] * n
    done = [False] * n
    rnd = 0
    while not all(done):
        for i in reversed(range(min(rnd + 1, n))):
            if not done[i]:
                try:
                    next(gens[i])
                except StopIteration as stop:
                    results[i] = stop.value
                    done[i] = True
        rnd += 1
    return results


def _wkv_chunk_gen(r, ld, k_raw, v, al, k_k, k_a, r_k, m):
    c = CHUNK
    kkx = k_raw * k_k
    k = k_raw * (1.0 + (al - 1.0) * k_a)
    h1 = ld.astype(BF)
    r1 = ld - h1.astype(F32)
    h2 = r1.astype(BF)
    h3 = (r1 - h2.astype(F32)).astype(BF)
    cum3 = jnp.dot(m["ltri"], jnp.concatenate([h1, h2, h3], axis=1), preferred_element_type=F32)
    sums = yield jnp.concatenate([kkx * kkx, r * k * r_k], axis=0)
    sq, bonus_s = sums[:c], sums[c:]

    kk = kkx / jnp.maximum(jnp.sqrt(sq), 1e-12)
    a = -kk
    b = kk * al
    cum = cum3[:, :LANES] + cum3[:, LANES:2 * LANES] + cum3[:, 2 * LANES:]
    mid = cum[c // 2 - 1:c // 2, :]
    tot = cum[c - 1:c, :]
    e_in = jnp.exp(cum - mid)
    e_ex = jnp.exp(cum - ld - mid)
    e_inv = jnp.exp(mid - cum)
    e_mid = jnp.exp(mid)
    e_tm = jnp.exp(tot - mid)
    at, rt, bt, kt = a * e_ex, r * e_in, b * e_inv, k * e_inv
    a0, r0, bh, kh = at * e_mid, rt * e_mid, bt * e_tm, kt * e_tm
    stack = lambda x: _tile_rows(x, 2, m["bd64"])
    quad = _bdot_nt(jnp.concatenate([at, rt], axis=0),
                    jnp.concatenate([stack(bt), stack(kt)], axis=0))
    yield

    q_ab = quad[:c, :LANES]
    aak = jnp.where(m["strict"], quad[:c, LANES:], 0.0)
    mrb = jnp.where(m["incl"], quad[c:, :LANES], 0.0)
    mrk = jnp.where(m["incl"], quad[c:, LANES:], 0.0)
    ad = jnp.where(m["d16s"], q_ab, 0.0)
    p1 = ad[0:16] + ad[16:32] + ad[32:48] + ad[48:64]
    full16 = lambda x: _tile_rows(x, 8, m["bd16"])
    p = _bdot(p1, full16(p1))
    vst = stack(v)
    x1 = _bdot(aak, vst)
    yield
    t16 = p1 + jnp.where(m["eye16"], 1.0, 0.0)
    for _ in range(2):
        both = _bdot(jnp.concatenate([t16, p], axis=0), full16(p))
        yield
        t16 = t16 + both[:16]
        p = both[16:]
    t16n = _bdot(t16, full16(p))
    yield
    t16 = t16 + t16n
    t32d = _tile_rows(t16, 2, m["c32"])
    a1 = jnp.where(m["off1"], q_ab, 0.0)
    n1 = _bdot(a1[0:32] + a1[32:64], _tile_rows(t32d, 4, m["bd32"]))
    yield
    t32n = _bdot(t32d, _tile_rows(n1, 4, m["bd32"]))
    yield
    t64d = _tile_rows(t32d + t32n, 2, m["c64"])
    n2 = _bdot(jnp.where(m["off2"], q_ab, 0.0), stack(t64d))
    yield
    t64n = _bdot(t64d, stack(n2))
    yield
    tc = t64d + t64n
    wu = _bdot(tc, jnp.concatenate([stack(a0), stack(x1)], axis=1))
    yield
    w, u0 = wu[:, :LANES], wu[:, LANES:]
    qp = r0 + _bdot(mrb, stack(w))
    o0 = _bdot(jnp.concatenate([mrb, mrk], axis=1), jnp.concatenate([stack(u0), vst], axis=0))
    gh = _bdot_tn(jnp.concatenate([bh, kh], axis=0),
                  jnp.concatenate([jnp.concatenate([w, u0], axis=1),
                                   jnp.concatenate([jnp.zeros_like(v), v], axis=1)], axis=0))
    yield
    g = jnp.where(m["bd64"], gh[:, :LANES], 0.0) + jnp.where(m["eye"], jnp.exp(tot), 0.0)
    hh = jnp.where(m["bd64"], gh[:, LANES:], 0.0)
    return qp, o0, g, hh, bonus_s * v


def _wkv_chain_gen(state, steps):
    outs = []
    for qp, o0, g, hh in steps:
        both = _bdot(jnp.concatenate([qp, g], axis=0), state)
        outs.append(both[:CHUNK] + o0)
        state = both[CHUNK:] + hh
        yield
    return state, outs


def _wkv_norm_gen(o, bonus, lnx_g, lnx_b, gate, m):
    mean_s = yield o
    dev = o - mean_s * (1.0 / HEAD)
    var_s = yield dev * dev
    y = dev * lax.rsqrt(var_s * (1.0 / HEAD) + LNX_EPS) * lnx_g + lnx_b + bonus
    return (y * gate).astype(BF)


def _wkv_body(r_ref, k_ref, v_ref, ld_ref, al_ref, g_ref, kk_ref, ka_ref, rk_ref, lg_ref, lb_ref,
              y_ref, state_ref):
    t = pl.program_id(2)

    @pl.when(t == 0)
    def _():
        state_ref[...] = jnp.zeros_like(state_ref)

    m = _wkv_masks()
    n_chunks = r_ref.shape[1] // CHUNK
    n_pairs = r_ref.shape[2] // LANES
    work = [(p, ci) for p in range(n_pairs) for ci in range(n_chunks)]
    rows = lambda ci: pl.ds(ci * CHUNK, CHUNK)
    cols = lambda p: pl.ds(p * LANES, LANES)

    gens = []
    for p, ci in work:
        gens.append(_wkv_chunk_gen(
            r_ref[0, rows(ci), cols(p)].astype(F32), ld_ref[0, rows(ci), cols(p)],
            k_ref[0, rows(ci), cols(p)], v_ref[0, rows(ci), cols(p)].astype(F32),
            al_ref[0, rows(ci), cols(p)], kk_ref[:, cols(p)], ka_ref[:, cols(p)], rk_ref[:, cols(p)], m))
    seg_sums = functools.partial(_seg_sum, m=m)
    pre = _interleave(gens, seg_sums)

    chains = _interleave([
        _wkv_chain_gen(state_ref[p], [pre[p * n_chunks + ci][:4] for ci in range(n_chunks)])
        for p in range(n_pairs)])
    for p in range(n_pairs):
        state_ref[p] = chains[p][0]

    ys = _interleave([
        _wkv_norm_gen(chains[p][1][ci], pre[p * n_chunks + ci][4], lg_ref[:, cols(p)], lb_ref[:, cols(p)],
                      g_ref[0, rows(ci), cols(p)].astype(F32), m)
        for p, ci in work], seg_sums)
    for (p, ci), y in zip(work, ys):
        y_ref[0, rows(ci), cols(p)] = y


def _wkv(r, k, v, ld, al, g, k_k, k_a, r_k, lnx_g, lnx_b):
    b, s, d = r.shape
    tc = WKV_TILE
    width = WKV_PAIRS * LANES
    blk = pl.BlockSpec((1, tc, width), lambda bi, hp, t: (bi, t, hp))
    par = pl.BlockSpec((1, width), lambda bi, hp, t: (0, hp))
    return pl.pallas_call(
        _wkv_body,
        grid=(b, d // width, s // tc),
        in_specs=[blk] * 6 + [par] * 5,
        out_specs=blk,
        out_shape=jax.ShapeDtypeStruct((b, s, d), BF),
        scratch_shapes=[pltpu.VMEM((WKV_PAIRS, LANES, LANES), F32)],
        compiler_params=_params(("parallel", "parallel", "arbitrary")),
        name="wkv",
    )(r, k, v, ld, al, g, k_k, k_a, r_k, lnx_g, lnx_b)


def _ffn(x1, fng, wgu_ref, wdown_ref):
    h = _rms(x1, fng).astype(BF)

    def ff_chunk(lo, hi):
        gate = jnp.dot(h, wgu_ref[:, lo:hi], preferred_element_type=F32)
        up = jnp.dot(h, wgu_ref[:, D_FF + lo:D_FF + hi], preferred_element_type=F32)
        yield
        act = (gate * jax.nn.sigmoid(gate) * up).astype(BF)
        return jnp.dot(act, wdown_ref[lo:hi, :], preferred_element_type=F32)

    acc = x1
    for part in _pipeline([ff_chunk(lo, hi) for lo, hi in zip(FF_SPLITS[:-1], FF_SPLITS[1:])]):
        acc = acc + part
    return acc


def _oproj_ffn_body(x_ref, y_ref, wo_ref, fng_ref, wgu_ref, wdown_ref, out_ref):
    x1 = x_ref[...] + jnp.dot(y_ref[...], wo_ref[...], preferred_element_type=F32)
    out_ref[...] = _ffn(x1, fng_ref[...], wgu_ref, wdown_ref)


def _oproj_ffn(x2d, y2d, wo, fng, wgu, wdown):
    m, d = x2d.shape
    tm = ROW_TILE
    row = pl.BlockSpec((tm, d), lambda i: (i, 0))
    consts = [wo, fng, wgu, wdown]
    return pl.pallas_call(
        _oproj_ffn_body,
        grid=(m // tm,),
        in_specs=[row, row] + [_const_spec(a.shape) for a in consts],
        out_specs=row,
        out_shape=jax.ShapeDtypeStruct((m, d), F32),
        compiler_params=_params(("parallel",)),
        name="oproj_ffn",
    )(x2d, y2d, *consts)


def _token_order(ref, perm_ref):
    dilation, n = ref.shape[1], ref.shape[2]
    if dilation == 1:
        return ref[0, 0].astype(F32)
    n_blocks = ref.shape[3] // LANES
    for r in range(dilation):
        for cb in range(n_blocks):
            perm_ref[cb, pl.ds(r, n, stride=dilation), :] = ref[0, r, :, pl.ds(cb * LANES, LANES)].astype(F32)
    return jnp.concatenate([perm_ref[cb] for cb in range(n_blocks)], axis=1)


def _merge_oproj_ffn_body(x_ref, o1_ref, o2_ref, o3_ref, l1_ref, l2_ref, l3_ref, ex_ref,
                          wo_ref, fng_ref, wgu_ref, wdown_ref, out_ref, *scratch):
    o_perm, l_perm = scratch[:N_GROUPS], scratch[N_GROUPS:]
    l1, l2, l3 = (_token_order(ref, s) for ref, s in zip((l1_ref, l2_ref, l3_ref), l_perm))
    mx = jnp.maximum(jnp.maximum(l1, l2), l3)
    e1, e2, e3 = jnp.exp(l1 - mx), jnp.exp(l2 - mx), jnp.exp(l3 - mx)
    inv = 1.0 / (e1 + e2 + e3)
    o = jnp.zeros(x_ref.shape, F32)
    for e, o_ref, s in zip((e1, e2, e3), (o1_ref, o2_ref, o3_ref), o_perm):
        hi, lo = _hi_lo(e * inv)
        alpha = jnp.dot(jnp.concatenate([hi, lo], axis=1), ex_ref[...], preferred_element_type=F32)
        o = o + alpha * _token_order(o_ref, s)
    x1 = x_ref[...] + _bdot(o, wo_ref[...])
    out_ref[...] = _ffn(x1, fng_ref[...], wgu_ref, wdown_ref)


def _merge_oproj_ffn(x2d, seq, os_, ls_, expand, wo, fng, wgu, wdown):
    m, d = x2d.shape
    tm = ROW_TILE
    tiles_per_seq = seq // tm
    row = pl.BlockSpec((tm, d), lambda i: (i, 0))
    dil = lambda a: pl.BlockSpec((1, a.shape[1], tm // a.shape[1], a.shape[3]),
                                 lambda i: (i // tiles_per_seq, 0, i % tiles_per_seq, 0))
    consts = [expand, wo, fng, wgu, wdown]
    return pl.pallas_call(
        _merge_oproj_ffn_body,
        grid=(m // tm,),
        in_specs=[row] + [dil(a) for a in os_] + [dil(a) for a in ls_] + [_const_spec(a.shape) for a in consts],
        out_specs=row,
        out_shape=jax.ShapeDtypeStruct((m, d), F32),
        scratch_shapes=([pltpu.VMEM((d // LANES, tm, LANES), F32)] * N_GROUPS
                        + [pltpu.VMEM((1, tm, LANES), F32)] * N_GROUPS),
        compiler_params=_params(("parallel",)),
        name="merge_oproj_ffn",
    )(x2d, *os_, *ls_, *consts)


PROJ_CHUNK = 256
SEG_WIDTH = 256


def _head_proj_body(n_rope, x_ref, ng_ref, w_ref, cos_ref, sin_ref, seg_ref, *rest):
    out_refs, perm_ref = rest[:-1], rest[-1]
    tm = x_ref.shape[0]
    h = _rms(x_ref[...], ng_ref[...]).astype(BF)

    def column_chunk(lo):
        y = jnp.dot(h, w_ref[:, lo:lo + PROJ_CHUNK], preferred_element_type=F32)
        yield
        out_ref = out_refs[lo // D_MODEL]
        dilation = out_ref.shape[1]
        if lo < n_rope:
            group = lo // D_MODEL
            ss = jnp.concatenate(
                [jnp.dot((ys * ys).astype(BF), seg_ref[...], preferred_element_type=F32)
                 for ys in (y[:, s:s + SEG_WIDTH] for s in range(0, PROJ_CHUNK, SEG_WIDTH))], axis=1)
            yield
            y = y * lax.rsqrt(ss * (1.0 / HEAD) + RMS_EPS)
        for j in range(PROJ_CHUNK // LANES):
            res = y[:, j * LANES:(j + 1) * LANES]
            col = lo + j * LANES
            if lo < n_rope:
                res = res * cos_ref[group] + pltpu.roll(res, LANES // 2, 1) * sin_ref[group]
            ocol = pl.ds(col % D_MODEL, LANES)
            if dilation == 1:
                out_ref[0, 0, :, ocol] = res.astype(BF)
                continue
            perm_ref[j] = res
            for r in range(dilation):
                out_ref[0, r, :, ocol] = perm_ref[j, pl.ds(r, tm // dilation, stride=dilation), :].astype(BF)

    _pipeline([column_chunk(c * PROJ_CHUNK) for c in range(len(out_refs) * D_MODEL // PROJ_CHUNK)])


def _head_proj(x2d, bsz, seq, ng, w, n_rope, cos_t, sin_t, seg):
    m, d = x2d.shape
    n_arrays = w.shape[1] // d
    tm = ROW_TILE
    tiles_per_seq = seq // tm
    row = pl.BlockSpec((tm, d), lambda i: (i, 0))
    tab = pl.BlockSpec((N_GROUPS, tm, LANES), lambda i: (0, i % tiles_per_seq, 0))
    dils = [DILATED_GROUPS[a % N_GROUPS][1] for a in range(n_arrays)]
    return pl.pallas_call(
        functools.partial(_head_proj_body, n_rope),
        grid=(m // tm,),
        in_specs=[row, _const_spec(ng.shape), _const_spec(w.shape), tab, tab, _const_spec(seg.shape)],
        out_specs=[pl.BlockSpec((1, dl, tm // dl, d),
                                lambda i: (i // tiles_per_seq, 0, i % tiles_per_seq, 0)) for dl in dils],
        out_shape=[jax.ShapeDtypeStruct((bsz, dl, seq // dl, d), BF) for dl in dils],
        scratch_shapes=[pltpu.VMEM((PROJ_CHUNK // LANES, tm, LANES), F32)],
        compiler_params=_params(("parallel",)),
        name="head_proj",
    )(x2d, ng, w, cos_t, sin_t, seg)


def _attn_body(q_ref, kp_ref, kc_ref, vp_ref, vc_ref, o_ref, l_ref):
    n = pl.program_id(2)
    blk = ATT_BLOCK
    row = _iota((blk, 2 * blk), 0)
    col = _iota((blk, 2 * blk), 1)
    own_bias = jnp.where(col - blk <= row, 0.0, NEG_BIG)
    bias_inner = jnp.where(col >= blk, own_bias, jnp.where(col >= row, 0.0, NEG_BIG))
    prev_bias = jnp.where(n > 0, 0.0, NEG_BIG)
    bias_first = jnp.where(col >= blk, own_bias, jnp.where(col >= row, prev_bias, NEG_BIG))
    lane = _iota((blk, LANES), 1)
    left = lane < HEAD
    left_kv = _iota((2 * blk, LANES), 1) < HEAD
    q_first = (lane & (HEAD // 2)) == 0

    def window(prev_ref, cur_ref, sub, cols):
        if sub == 0:
            return jnp.concatenate([prev_ref[0, 0, :, cols], cur_ref[0, 0, 0:blk, cols]], axis=0)
        return cur_ref[0, 0, (sub - 1) * blk:(sub + 1) * blk, cols]

    def head_pair(sub, j):
        cols = slice(j * LANES, (j + 1) * LANES)
        rows = slice(sub * blk, (sub + 1) * blk)
        q = q_ref[0, 0, rows, cols]
        kcat = window(kp_ref, kc_ref, sub, cols)
        scores = [lax.dot_general(jnp.where(mask, q, jnp.zeros_like(q)), kcat, (((1,), (1,)), ((), ())),
                                  preferred_element_type=F32) for mask in (q_first, ~q_first)]
        yield
        vcat = window(vp_ref, vc_ref, sub, cols)
        bias = bias_first if sub == 0 else bias_inner
        scores = [s + bias for s in scores]
        mxs = [jnp.max(s, axis=-1, keepdims=True) for s in scores]
        yield
        one = jnp.ones_like(vcat)
        pvs = [jnp.dot(jnp.exp2((s - mx).astype(BF)), jnp.where(own, vcat, one), preferred_element_type=F32)
               for s, mx, own in zip(scores, mxs, (left_kv, ~left_kv))]
        yield
        den = pltpu.roll(jnp.where(left, pvs[1], pvs[0]), HEAD, 1)
        o_ref[0, 0, rows, cols] = (jnp.where(left, pvs[0], pvs[1]) / den).astype(BF)
        return (jnp.where(left, mxs[0], mxs[1]) + jnp.log2(den)) * LN2

    n_pairs = q_ref.shape[3] // LANES
    n_sub = q_ref.shape[2] // blk
    lses = _interleave([head_pair(sub, j) for sub in range(n_sub) for j in range(n_pairs)])
    for sub in range(n_sub):
        lse_tile = jnp.zeros((blk, LANES), F32)
        for j in range(n_pairs):
            lse_tile = jnp.where((lane & (HEAD - 1)) == j, lses[sub * n_pairs + j], lse_tile)
        l_ref[0, 0, sub * blk:(sub + 1) * blk, :] = lse_tile


def _attn(q, k, v):
    b, dilation, ln, d = q.shape
    blk = ATT_BLOCK
    step = ATT_STEP * blk
    cur = pl.BlockSpec((1, 1, step, d), lambda bi, r, n: (bi, r, n, 0))
    prev = pl.BlockSpec((1, 1, blk, d), lambda bi, r, n: (bi, r, jnp.maximum(ATT_STEP * n - 1, 0), 0))
    return pl.pallas_call(
        _attn_body,
        grid=(b, dilation, ln // step),
        in_specs=[cur, prev, cur, prev, cur],
        out_specs=[cur, pl.BlockSpec((1, 1, step, LANES), lambda bi, r, n: (bi, r, n, 0))],
        out_shape=[jax.ShapeDtypeStruct((b, dilation, ln, d), BF),
                   jax.ShapeDtypeStruct((b, dilation, ln, LANES), F32)],
        compiler_params=_params(("parallel", "parallel", "arbitrary")),
        name=f"attn_d{dilation}",
    )(q, k, k, v, v)


def _rotary_columns(w):
    k, n = w.shape
    half = HEAD // 2
    return w.reshape(k, n // LANES, 2, 2, half).transpose(0, 1, 3, 2, 4).reshape(k, n)


def _rope_tables(seq, gains, scale):
    half = HEAD // 2
    inv = ROPE_THETA ** (-jnp.arange(0, HEAD, 2, dtype=F32) / HEAD)
    ang = jnp.arange(seq, dtype=F32)[:, None] * inv[None, :]
    cos, sin = jnp.cos(ang), jnp.sin(ang)
    cos_t = jnp.concatenate([cos, cos, cos, cos], axis=1)
    sin_t = jnp.concatenate([-sin, -sin, sin, sin], axis=1)
    gains = gains.astype(F32) * scale
    lo, hi = gains[:, :half], gains[:, half:]
    own = jnp.concatenate([lo, lo, hi, hi], axis=1)
    partner = jnp.concatenate([hi, hi, lo, lo], axis=1)
    return cos_t[None] * own[:, None, :], sin_t[None] * partner[:, None, :]


def kernel(x, a_norm_g, a_mu, a_w_r, a_w_k, a_w_v, a_w0, a_w1, a_w2, a_a0, a_a1, a_a2, a_v0, a_v1, a_v2, a_g1, a_g2, a_k_k, a_k_a, a_r_k, a_lnx_g, a_lnx_b, a_w_o, kv_norm_g, w_kv, k_norm_g, b_norm_g, b_w_q, b_q_norm_g, b_w_o, f_norm_g, f_w_gu, f_w_down):
    bsz, seq, d = x.shape
    assert d == D_MODEL and seq % (ATT_STEP * ATT_BLOCK * DILATED_GROUPS[-1][1]) == 0
    assert seq % ROW_TILE == 0 and seq % WKV_TILE == 0
    n_a = a_norm_g.shape[0]
    n_b = b_norm_g.shape[0]
    row1 = lambda t: t.reshape(1, -1).astype(F32)
    bf = lambda t: t.astype(BF)

    seg_r, seg_c = _iota((SEG_WIDTH, SEG_WIDTH), 0), _iota((SEG_WIDTH, SEG_WIDTH), 1)
    seg = jnp.where(((seg_r >> 7) == (seg_c >> 7)) & (((seg_r ^ seg_c) & (HEAD // 2)) == 0),
                    1.0, 0.0).astype(BF)
    ex_row, ex_col = _iota((LANES, d), 0), _iota((LANES, d), 1)
    ex_head = 2 * (ex_row & (HEAD - 1)) + (ex_row >> 6)
    ex = jnp.where(((ex_row & (HEAD - 1)) < N_PAIRS) & (ex_head == (ex_col >> 6)), 1.0, 0.0).astype(BF)
    expand = jnp.concatenate([ex, ex], axis=0)

    x2d = x.reshape(bsz * seq, d)
    v_first = None
    for i in range(n_a):
        vmix = None if i == 0 else (row1(a_v0[i - 1]), bf(a_v1[i - 1]), bf(a_v2[i - 1]))
        mu = jnp.pad(a_mu[i], ((0, 2), (0, 0)))
        r, k, v, ld, al, g = _rwkv_proj(
            x2d, v_first, seq, row1(a_norm_g[i]), mu, bf(a_w_r[i]), bf(a_w_k[i]), bf(a_w_v[i]),
            row1(a_w0[i]), bf(a_w1[i]), bf(a_w2[i]), row1(a_a0[i]), bf(a_a1[i]), bf(a_a2[i]),
            vmix, bf(a_g1[i]), bf(a_g2[i]))
        if i == 0:
            v_first = v
        to3 = lambda t: t.reshape(bsz, seq, d)
        y = _wkv(to3(r), to3(k), to3(v), to3(ld), to3(al), to3(g), row1(a_k_k[i]), row1(a_k_a[i]),
                 row1(a_r_k[i]), row1(a_lnx_g[i]), row1(a_lnx_b[i]))
        x2d = _oproj_ffn(x2d, y.reshape(bsz * seq, d), bf(a_w_o[i]), row1(f_norm_g[i]),
                         bf(f_w_gu[i]), bf(f_w_down[i]))

    w_kv_r = jnp.concatenate([_rotary_columns(bf(w_kv[:, :Q_WIDTH])), bf(w_kv[:, Q_WIDTH:])], axis=1)
    kvs = _head_proj(x2d, bsz, seq, row1(kv_norm_g), w_kv_r, Q_WIDTH,
                     *_rope_tables(seq, k_norm_g, 1.0), seg)
    for j in range(n_b):
        qs = _head_proj(x2d, bsz, seq, row1(b_norm_g[j]), _rotary_columns(bf(b_w_q[j])), Q_WIDTH,
                        *_rope_tables(seq, b_q_norm_g[j], HEAD ** -0.5 * LOG2E), seg)
        outs, lses = [], []
        for gi in range(N_GROUPS):
            o, lse = _attn(qs[gi], kvs[gi], kvs[N_GROUPS + gi])
            outs.append(o)
            lses.append(lse)
        layer = n_a + j
        x2d = _merge_oproj_ffn(x2d, seq, outs, lses, expand, bf(b_w_o[j]), row1(f_norm_g[layer]),
                               bf(f_w_gu[layer]), bf(f_w_down[layer]))
    return x2d.reshape(bsz, seq, d)
```

```python
import functools

import jax
import jax.numpy as jnp
from jax import lax
from jax.experimental import pallas as pl
from jax.experimental.pallas import tpu as pltpu

F32 = jnp.float32
BF = jnp.bfloat16

D_MODEL = 1024
RMS_EPS = 1e-6
LNX_EPS = 64e-5
HEAD = 64
LANES = 128
N_PAIRS = D_MODEL // LANES
CHUNK = 64
DILATED_GROUPS = ((128, 1), (512, 4), (2048, 16))
N_GROUPS = len(DILATED_GROUPS)
ATT_BLOCK = 128
ATT_STEP = 4
ROPE_THETA = 10000.0
Q_WIDTH = N_GROUPS * D_MODEL
D_FF = 2816
MXU_WIDTH = 256
FF_SPLITS = (0, D_FF)
NEG_BIG = -1e30
DECAY_SCALE = 0.6065306597126334
LN2 = 0.6931471805599453
LOG2E = 1.4426950408889634

ROW_TILE = 512
WKV_TILE = 256
WKV_PAIRS = 8
VMEM_LIMIT = 56 * 1024 * 1024


def _iota(shape, dim):
    return lax.broadcasted_iota(jnp.int32, shape, dim)


def _const_spec(shape):
    nd = len(shape)
    return pl.BlockSpec(shape, lambda *_: (0,) * nd, pipeline_mode=pl.Buffered(1))


def _params(sem):
    return pltpu.CompilerParams(dimension_semantics=sem, vmem_limit_bytes=VMEM_LIMIT)


def _bdot(a, b):
    return jnp.dot(a.astype(BF), b.astype(BF), preferred_element_type=F32)


def _bdot_nt(a, b):
    return lax.dot_general(a.astype(BF), b.astype(BF), (((1,), (1,)), ((), ())),
                           preferred_element_type=F32)


def _bdot_tn(a, b):
    return lax.dot_general(a.astype(BF), b.astype(BF), (((0,), (0,)), ((), ())),
                           preferred_element_type=F32)


def _rms(x, g):
    ms = jnp.mean(x * x, axis=-1, keepdims=True)
    return x * lax.rsqrt(ms + RMS_EPS) * g


def _hi_lo(x):
    hi = x.astype(BF)
    lo = (x - hi.astype(F32)).astype(BF)
    return hi, lo


def _rwkv_proj_body(has_vmix, tiles_per_seq, *refs):
    if has_vmix:
        (x_ref, vf_ref, ng_ref, mu_ref, wr_ref, wk_ref, wv_ref, w0_ref, w1_ref, w2_ref,
         a0_ref, a1_ref, a2_ref, v0_ref, v1_ref, v2_ref, g1_ref, g2_ref,
         r_out, k_out, v_out, ld_out, al_out, g_out, carry_ref) = refs
    else:
        (x_ref, ng_ref, mu_ref, wr_ref, wk_ref, wv_ref, w0_ref, w1_ref, w2_ref,
         a0_ref, a1_ref, a2_ref, g1_ref, g2_ref,
         r_out, k_out, v_out, ld_out, al_out, g_out, carry_ref) = refs
    i = pl.program_id(0)
    tm = x_ref.shape[0]
    h = _rms(x_ref[...], ng_ref[...])

    @pl.when(i % tiles_per_seq == 0)
    def _():
        carry_ref[...] = jnp.zeros_like(carry_ref)

    prev_last = carry_ref[7:8, :]
    hprev = jnp.where(_iota(h.shape, 0) == 0, prev_last, pltpu.roll(h, 1, 0))
    carry_ref[...] = h[tm - 8:tm, :]
    dx = hprev - h
    mu = mu_ref[...]
    mix = lambda j: (h + dx * mu[j:j + 1, :]).astype(BF)
    xr, xw, xk, xv, xa, xg = (mix(j) for j in range(6))

    def lora(xin, w_in_ref, act, w_out_ref):
        hidden = _bdot(xin, w_in_ref[...])
        yield
        return _bdot(act(hidden), w_out_ref[...])

    def full(xin, w_ref):
        return _bdot(xin, w_ref[...])
        yield

    ident = lambda t: t
    gens = [lora(xw, w1_ref, jnp.tanh, w2_ref), lora(xa, a1_ref, ident, a2_ref),
            lora(xg, g1_ref, jax.nn.sigmoid, g2_ref)]
    if has_vmix:
        gens.append(lora(xv, v1_ref, ident, v2_ref))
    gens += [full(xr, wr_ref), full(xk, wk_ref), full(xv, wv_ref)]
    res = _interleave(gens)
    lw, la, lg = res[:3]
    r, k, v = res[-3:]

    r_out[...] = r.astype(BF)
    k_out[...] = k
    if has_vmix:
        v = v + (vf_ref[...].astype(F32) - v) * jax.nn.sigmoid(v0_ref[...] + res[3])
    v_out[...] = v.astype(BF)
    ld_out[...] = -DECAY_SCALE * jax.nn.sigmoid(w0_ref[...] + lw)
    al_out[...] = jax.nn.sigmoid(a0_ref[...] + la)
    g_out[...] = lg.astype(BF)


def _rwkv_proj(x2d, v_first, seq, ng, mu, wr, wk, wv, w0, w1, w2, a0, a1, a2, vmix, g1, g2):
    m, d = x2d.shape
    tm = ROW_TILE
    has_vmix = vmix is not None
    row = pl.BlockSpec((tm, d), lambda i: (i, 0))
    ins = [x2d] + ([v_first] if has_vmix else []) + [ng, mu, wr, wk, wv, w0, w1, w2, a0, a1, a2]
    ins += (list(vmix) if has_vmix else []) + [g1, g2]
    n_row = 2 if has_vmix else 1
    specs = [row] * n_row + [_const_spec(a.shape) for a in ins[n_row:]]
    out_dt = (BF, F32, BF, F32, F32, BF)
    return pl.pallas_call(
        functools.partial(_rwkv_proj_body, has_vmix, seq // tm),
        grid=(m // tm,),
        in_specs=specs,
        out_specs=[row] * 6,
        out_shape=[jax.ShapeDtypeStruct((m, d), t) for t in out_dt],
        scratch_shapes=[pltpu.VMEM((8, d), F32)],
        compiler_params=_params(("arbitrary",)),
        name="rwkv_proj",
    )(*ins)


def _wkv_masks():
    c = CHUNK
    row = _iota((c, LANES), 0)
    rr = _iota((c, LANES), 1) & (c - 1)
    r128 = _iota((LANES, LANES), 0)
    l128 = _iota((LANES, LANES), 1)
    m = {}
    m["strict"] = row > rr
    m["incl"] = row >= rr
    m["d16s"] = ((row >> 4) == (rr >> 4)) & (row > rr)
    m["off1"] = ((row >> 5) == (rr >> 5)) & (((row >> 4) & 1) == 1) & (((rr >> 4) & 1) == 0)
    m["off2"] = ((row >> 5) == 1) & ((rr >> 5) == 0)
    m["bd16"] = (r128 >> 4) == (l128 >> 4)
    m["bd32"] = (r128 >> 5) == (l128 >> 5)
    m["bd64"] = (r128 >> 6) == (l128 >> 6)
    m["eye"] = r128 == l128
    m["eye16"] = _iota((16, LANES), 0) == (_iota((16, LANES), 1) & 15)
    m["c32"] = (_iota((32, LANES), 0) >> 4) == ((_iota((32, LANES), 1) & 31) >> 4)
    m["c64"] = (_iota((c, LANES), 0) >> 5) == (rr >> 5)
    m["ltri"] = jnp.where(_iota((c, c), 0) >= _iota((c, c), 1), 1.0, 0.0).astype(BF)
    seg = jnp.where(m["bd64"], 1.0, 0.0).astype(BF)
    m["seg"] = jnp.concatenate([seg, seg], axis=0)
    return m


def _seg_sum(x, m):
    hi, lo = _hi_lo(x)
    return jnp.dot(jnp.concatenate([hi, lo], axis=1), m["seg"], preferred_element_type=F32)


def _tile_rows(x, reps, mask):
    return jnp.where(mask, jnp.concatenate([x] * reps, axis=0), 0.0)


def _interleave(gens, batched=None):
    results = [None] * len(gens)
    live = list(range(len(gens)))
    replies = {}
    while live:
        still, asks = [], []
        for i in live:
            try:
                ask = gens[i].send(replies.pop(i, None))
                still.append(i)
                if ask is not None:
                    asks.append((i, ask))
            except StopIteration as stop:
                results[i] = stop.value
        if asks:
            out = batched(jnp.concatenate([a for _, a in asks], axis=0))
            row = 0
            for i, a in asks:
                replies[i] = out[row:row + a.shape[0]]
                row += a.shape[0]
        live = still
    return results


def _pipeline(gens):
    n = len(gens)
    results = [None] * n
    done = [False] * n
    rnd = 0
    while not all(done):
        for i in reversed(range(min(rnd + 1, n))):
            if not done[i]:
                try:
                    next(gens[i])
                except StopIteration as stop:
                    results[i] = stop.value
                    done[i] = True
        rnd += 1
    return results


def _wkv_chunk_gen(r, ld, k_raw, v, al, k_k, k_a, r_k, m):
    c = CHUNK
    kkx = k_raw * k_k
    k = k_raw * (1.0 + (al - 1.0) * k_a)
    h1 = ld.astype(BF)
    r1 = ld - h1.astype(F32)
    h2 = r1.astype(BF)
    h3 = (r1 - h2.astype(F32)).astype(BF)
    cum3 = jnp.dot(m["ltri"], jnp.concatenate([h1, h2, h3], axis=1), preferred_element_type=F32)
    sums = yield jnp.concatenate([kkx * kkx, r * k * r_k], axis=0)
    sq, bonus_s = sums[:c], sums[c:]

    kk = kkx / jnp.maximum(jnp.sqrt(sq), 1e-12)
    a = -kk
    b = kk * al
    cum = cum3[:, :LANES] + cum3[:, LANES:2 * LANES] + cum3[:, 2 * LANES:]
    mid = cum[c // 2 - 1:c // 2, :]
    tot = cum[c - 1:c, :]
    e_in = jnp.exp(cum - mid)
    e_ex = jnp.exp(cum - ld - mid)
    e_inv = jnp.exp(mid - cum)
    e_mid = jnp.exp(mid)
    e_tm = jnp.exp(tot - mid)
    at, rt, bt, kt = a * e_ex, r * e_in, b * e_inv, k * e_inv
    a0, r0, bh, kh = at * e_mid, rt * e_mid, bt * e_tm, kt * e_tm
    stack = lambda x: _tile_rows(x, 2, m["bd64"])
    quad = _bdot_nt(jnp.concatenate([at, rt], axis=0),
                    jnp.concatenate([stack(bt), stack(kt)], axis=0))
    yield

    q_ab = quad[:c, :LANES]
    aak = jnp.where(m["strict"], quad[:c, LANES:], 0.0)
    mrb = jnp.where(m["incl"], quad[c:, :LANES], 0.0)
    mrk = jnp.where(m["incl"], quad[c:, LANES:], 0.0)
    ad = jnp.where(m["d16s"], q_ab, 0.0)
    p1 = ad[0:16] + ad[16:32] + ad[32:48] + ad[48:64]
    full16 = lambda x: _tile_rows(x, 8, m["bd16"])
    p = _bdot(p1, full16(p1))
    vst = stack(v)
    x1 = _bdot(aak, vst)
    yield
    t16 = p1 + jnp.where(m["eye16"], 1.0, 0.0)
    for _ in range(2):
        both = _bdot(jnp.concatenate([t16, p], axis=0), full16(p))
        yield
        t16 = t16 + both[:16]
        p = both[16:]
    t16n = _bdot(t16, full16(p))
    yield
    t16 = t16 + t16n
    t32d = _tile_rows(t16, 2, m["c32"])
    a1 = jnp.where(m["off1"], q_ab, 0.0)
    n1 = _bdot(a1[0:32] + a1[32:64], _tile_rows(t32d, 4, m["bd32"]))
    yield
    t32n = _bdot(t32d, _tile_rows(n1, 4, m["bd32"]))
    yield
    t64d = _tile_rows(t32d + t32n, 2, m["c64"])
    n2 = _bdot(jnp.where(m["off2"], q_ab, 0.0), stack(t64d))
    yield
    t64n = _bdot(t64d, stack(n2))
    yield
    tc = t64d + t64n
    wu = _bdot(tc, jnp.concatenate([stack(a0), stack(x1)], axis=1))
    yield
    w, u0 = wu[:, :LANES], wu[:, LANES:]
    qp = r0 + _bdot(mrb, stack(w))
    o0 = _bdot(jnp.concatenate([mrb, mrk], axis=1), jnp.concatenate([stack(u0), vst], axis=0))
    gh = _bdot_tn(jnp.concatenate([bh, kh], axis=0),
                  jnp.concatenate([jnp.concatenate([w, u0], axis=1),
                                   jnp.concatenate([jnp.zeros_like(v), v], axis=1)], axis=0))
    yield
    g = jnp.where(m["bd64"], gh[:, :LANES], 0.0) + jnp.where(m["eye"], jnp.exp(tot), 0.0)
    hh = jnp.where(m["bd64"], gh[:, LANES:], 0.0)
    return qp, o0, g, hh, bonus_s * v


def _wkv_chain_gen(state, steps):
    outs = []
    for qp, o0, g, hh in steps:
        both = _bdot(jnp.concatenate([qp, g], axis=0), state)
        outs.append(both[:CHUNK] + o0)
        state = both[CHUNK:] + hh
        yield
    return state, outs


def _wkv_norm_gen(o, bonus, lnx_g, lnx_b, gate, m):
    mean_s = yield o
    dev = o - mean_s * (1.0 / HEAD)
    var_s = yield dev * dev
    y = dev * lax.rsqrt(var_s * (1.0 / HEAD) + LNX_EPS) * lnx_g + lnx_b + bonus
    return (y * gate).astype(BF)


def _wkv_body(r_ref, k_ref, v_ref, ld_ref, al_ref, g_ref, kk_ref, ka_ref, rk_ref, lg_ref, lb_ref,
              y_ref, state_ref):
    t = pl.program_id(2)

    @pl.when(t == 0)
    def _():
        state_ref[...] = jnp.zeros_like(state_ref)

    m = _wkv_masks()
    n_chunks = r_ref.shape[1] // CHUNK
    n_pairs = r_ref.shape[2] // LANES
    work = [(p, ci) for p in range(n_pairs) for ci in range(n_chunks)]
    rows = lambda ci: pl.ds(ci * CHUNK, CHUNK)
    cols = lambda p: pl.ds(p * LANES, LANES)

    gens = []
    for p, ci in work:
        gens.append(_wkv_chunk_gen(
            r_ref[0, rows(ci), cols(p)].astype(F32), ld_ref[0, rows(ci), cols(p)],
            k_ref[0, rows(ci), cols(p)], v_ref[0, rows(ci), cols(p)].astype(F32),
            al_ref[0, rows(ci), cols(p)], kk_ref[:, cols(p)], ka_ref[:, cols(p)], rk_ref[:, cols(p)], m))
    seg_sums = functools.partial(_seg_sum, m=m)
    pre = _interleave(gens, seg_sums)

    chains = _interleave([
        _wkv_chain_gen(state_ref[p], [pre[p * n_chunks + ci][:4] for ci in range(n_chunks)])
        for p in range(n_pairs)])
    for p in range(n_pairs):
        state_ref[p] = chains[p][0]

    ys = _interleave([
        _wkv_norm_gen(chains[p][1][ci], pre[p * n_chunks + ci][4], lg_ref[:, cols(p)], lb_ref[:, cols(p)],
                      g_ref[0, rows(ci), cols(p)].astype(F32), m)
        for p, ci in work], seg_sums)
    for (p, ci), y in zip(work, ys):
        y_ref[0, rows(ci), cols(p)] = y


def _wkv(r, k, v, ld, al, g, k_k, k_a, r_k, lnx_g, lnx_b):
    b, s, d = r.shape
    tc = WKV_TILE
    width = WKV_PAIRS * LANES
    blk = pl.BlockSpec((1, tc, width), lambda bi, hp, t: (bi, t, hp))
    par = pl.BlockSpec((1, width), lambda bi, hp, t: (0, hp))
    return pl.pallas_call(
        _wkv_body,
        grid=(b, d // width, s // tc),
        in_specs=[blk] * 6 + [par] * 5,
        out_specs=blk,
        out_shape=jax.ShapeDtypeStruct((b, s, d), BF),
        scratch_shapes=[pltpu.VMEM((WKV_PAIRS, LANES, LANES), F32)],
        compiler_params=_params(("parallel", "parallel", "arbitrary")),
        name="wkv",
    )(r, k, v, ld, al, g, k_k, k_a, r_k, lnx_g, lnx_b)


def _ffn(x1, fng, wgu_ref, wdown_ref):
    h = _rms(x1, fng).astype(BF)

    def ff_chunk(lo, hi):
        gate = jnp.dot(h, wgu_ref[:, lo:hi], preferred_element_type=F32)
        up = jnp.dot(h, wgu_ref[:, D_FF + lo:D_FF + hi], preferred_element_type=F32)
        yield
        act = (gate * jax.nn.sigmoid(gate) * up).astype(BF)
        return jnp.dot(act, wdown_ref[lo:hi, :], preferred_element_type=F32)

    chunks = [ff_chunk(lo, hi) for lo, hi in zip(FF_SPLITS[:-1], FF_SPLITS[1:])]
    acc = x1
    next(chunks[0])
    for c, chunk in enumerate(chunks):
        if c + 1 < len(chunks):
            next(chunks[c + 1])
        yield
        try:
            next(chunk)
        except StopIteration as stop:
            acc = acc + stop.value
    return acc


def _row_halves(tm):
    half = tm // 2
    return [slice(0, half), slice(half, tm)]


def _oproj_ffn_body(x_ref, y_ref, wo_ref, fng_ref, wgu_ref, wdown_ref, out_ref):
    def rows_gen(rs):
        proj = jnp.dot(y_ref[rs, :], wo_ref[...], preferred_element_type=F32)
        yield
        out_ref[rs, :] = yield from _ffn(x_ref[rs, :] + proj, fng_ref[...], wgu_ref, wdown_ref)

    _interleave([rows_gen(rs) for rs in _row_halves(x_ref.shape[0])])


def _oproj_ffn(x2d, y2d, wo, fng, wgu, wdown):
    m, d = x2d.shape
    tm = ROW_TILE
    row = pl.BlockSpec((tm, d), lambda i: (i, 0))
    consts = [wo, fng, wgu, wdown]
    return pl.pallas_call(
        _oproj_ffn_body,
        grid=(m // tm,),
        in_specs=[row, row] + [_const_spec(a.shape) for a in consts],
        out_specs=row,
        out_shape=jax.ShapeDtypeStruct((m, d), F32),
        compiler_params=_params(("parallel",)),
        name="oproj_ffn",
    )(x2d, y2d, *consts)


def _token_order(ref, perm_ref, rs):
    dilation = ref.shape[1]
    if dilation == 1:
        return ref[0, 0, rs, :].astype(F32)
    n_rows = (rs.stop - rs.start) // dilation
    ls = pl.ds(rs.start // dilation, n_rows)
    n_blocks = ref.shape[3] // LANES
    for r in range(dilation):
        for cb in range(n_blocks):
            perm_ref[cb, pl.ds(rs.start + r, n_rows, stride=dilation), :] = (
                ref[0, r, ls, pl.ds(cb * LANES, LANES)].astype(F32))
    return jnp.concatenate([perm_ref[cb, rs, :] for cb in range(n_blocks)], axis=1)


def _merge_oproj_ffn_body(x_ref, o1_ref, o2_ref, o3_ref, l1_ref, l2_ref, l3_ref, ex_ref,
                          wo_ref, fng_ref, wgu_ref, wdown_ref, out_ref, *scratch):
    o_perm, l_perm = scratch[:N_GROUPS], scratch[N_GROUPS:]

    def rows_gen(rs):
        l1, l2, l3 = (_token_order(ref, s, rs) for ref, s in zip((l1_ref, l2_ref, l3_ref), l_perm))
        mx = jnp.maximum(jnp.maximum(l1, l2), l3)
        es = [jnp.exp(l - mx) for l in (l1, l2, l3)]
        inv = 1.0 / (es[0] + es[1] + es[2])
        alphas = [jnp.dot(jnp.concatenate(_hi_lo(e * inv), axis=1), ex_ref[...], preferred_element_type=F32)
                  for e in es]
        yield
        o = sum(alpha * _token_order(o_ref, s, rs)
                for alpha, o_ref, s in zip(alphas, (o1_ref, o2_ref, o3_ref), o_perm))
        proj = _bdot(o, wo_ref[...])
        yield
        out_ref[rs, :] = yield from _ffn(x_ref[rs, :] + proj, fng_ref[...], wgu_ref, wdown_ref)

    _interleave([rows_gen(rs) for rs in _row_halves(x_ref.shape[0])])


def _merge_oproj_ffn(x2d, seq, os_, ls_, expand, wo, fng, wgu, wdown):
    m, d = x2d.shape
    tm = ROW_TILE
    tiles_per_seq = seq // tm
    row = pl.BlockSpec((tm, d), lambda i: (i, 0))
    dil = lambda a: pl.BlockSpec((1, a.shape[1], tm // a.shape[1], a.shape[3]),
                                 lambda i: (i // tiles_per_seq, 0, i % tiles_per_seq, 0))
    consts = [expand, wo, fng, wgu, wdown]
    return pl.pallas_call(
        _merge_oproj_ffn_body,
        grid=(m // tm,),
        in_specs=[row] + [dil(a) for a in os_] + [dil(a) for a in ls_] + [_const_spec(a.shape) for a in consts],
        out_specs=row,
        out_shape=jax.ShapeDtypeStruct((m, d), F32),
        scratch_shapes=([pltpu.VMEM((d // LANES, tm, LANES), F32)] * N_GROUPS
                        + [pltpu.VMEM((1, tm, LANES), F32)] * N_GROUPS),
        compiler_params=_params(("parallel",)),
        name="merge_oproj_ffn",
    )(x2d, *os_, *ls_, *consts)


PROJ_CHUNK = 256
SEG_WIDTH = 256


def _head_proj_body(n_rope, x_ref, ng_ref, w_ref, cos_ref, sin_ref, seg_ref, *rest):
    out_refs, perm_ref = rest[:-1], rest[-1]
    tm = x_ref.shape[0]
    h = _rms(x_ref[...], ng_ref[...]).astype(BF)

    def column_chunk(lo):
        y = jnp.dot(h, w_ref[:, lo:lo + PROJ_CHUNK], preferred_element_type=F32)
        yield
        out_ref = out_refs[lo // D_MODEL]
        dilation = out_ref.shape[1]
        if lo < n_rope:
            group = lo // D_MODEL
            ss = jnp.concatenate(
                [jnp.dot((ys * ys).astype(BF), seg_ref[...], preferred_element_type=F32)
                 for ys in (y[:, s:s + SEG_WIDTH] for s in range(0, PROJ_CHUNK, SEG_WIDTH))], axis=1)
            yield
            y = y * lax.rsqrt(ss * (1.0 / HEAD) + RMS_EPS)
        for j in range(PROJ_CHUNK // LANES):
            res = y[:, j * LANES:(j + 1) * LANES]
            col = lo + j * LANES
            if lo < n_rope:
                res = res * cos_ref[group] + pltpu.roll(res, LANES // 2, 1) * sin_ref[group]
            ocol = pl.ds(col % D_MODEL, LANES)
            if dilation == 1:
                out_ref[0, 0, :, ocol] = res.astype(BF)
                continue
            perm_ref[j] = res
            for r in range(dilation):
                out_ref[0, r, :, ocol] = perm_ref[j, pl.ds(r, tm // dilation, stride=dilation), :].astype(BF)

    _pipeline([column_chunk(lo) for lo in range(0, len(out_refs) * D_MODEL, PROJ_CHUNK)])


def _head_proj(x2d, bsz, seq, ng, w, n_rope, cos_t, sin_t, seg):
    m, d = x2d.shape
    n_arrays = w.shape[1] // d
    tm = ROW_TILE
    tiles_per_seq = seq // tm
    row = pl.BlockSpec((tm, d), lambda i: (i, 0))
    tab = pl.BlockSpec((N_GROUPS, tm, LANES), lambda i: (0, i % tiles_per_seq, 0))
    dils = [DILATED_GROUPS[a % N_GROUPS][1] for a in range(n_arrays)]
    return pl.pallas_call(
        functools.partial(_head_proj_body, n_rope),
        grid=(m // tm,),
        in_specs=[row, _const_spec(ng.shape), _const_spec(w.shape), tab, tab, _const_spec(seg.shape)],
        out_specs=[pl.BlockSpec((1, dl, tm // dl, d),
                                lambda i: (i // tiles_per_seq, 0, i % tiles_per_seq, 0)) for dl in dils],
        out_shape=[jax.ShapeDtypeStruct((bsz, dl, seq // dl, d), BF) for dl in dils],
        scratch_shapes=[pltpu.VMEM((PROJ_CHUNK // LANES, tm, LANES), F32)],
        compiler_params=_params(("parallel",)),
        name="head_proj",
    )(x2d, ng, w, cos_t, sin_t, seg)


def _attn_body(q_ref, kp_ref, kc_ref, vp_ref, vc_ref, o_ref, l_ref):
    n = pl.program_id(2)
    blk = ATT_BLOCK
    row = _iota((blk, 2 * blk), 0)
    col = _iota((blk, 2 * blk), 1)
    own_bias = jnp.where(col - blk <= row, 0.0, NEG_BIG)
    bias_inner = jnp.where(col >= blk, own_bias, jnp.where(col >= row, 0.0, NEG_BIG))
    prev_bias = jnp.where(n > 0, 0.0, NEG_BIG)
    bias_first = jnp.where(col >= blk, own_bias, jnp.where(col >= row, prev_bias, NEG_BIG))
    lane = _iota((blk, LANES), 1)
    left = lane < HEAD
    left_kv = _iota((2 * blk, LANES), 1) < HEAD
    q_first = (lane & (HEAD // 2)) == 0

    def window(prev_ref, cur_ref, sub, cols):
        if sub == 0:
            return jnp.concatenate([prev_ref[0, 0, :, cols], cur_ref[0, 0, 0:blk, cols]], axis=0)
        return cur_ref[0, 0, (sub - 1) * blk:(sub + 1) * blk, cols]

    def head_pair(sub, j):
        cols = slice(j * LANES, (j + 1) * LANES)
        rows = slice(sub * blk, (sub + 1) * blk)
        q = q_ref[0, 0, rows, cols]
        kcat = window(kp_ref, kc_ref, sub, cols)
        scores = [lax.dot_general(jnp.where(mask, q, jnp.zeros_like(q)), kcat, (((1,), (1,)), ((), ())),
                                  preferred_element_type=F32) for mask in (q_first, ~q_first)]
        yield
        vcat = window(vp_ref, vc_ref, sub, cols)
        bias = bias_first if sub == 0 else bias_inner
        scores = [s + bias for s in scores]
        mxs = [jnp.max(s, axis=-1, keepdims=True) for s in scores]
        yield
        one = jnp.ones_like(vcat)
        pvs = [jnp.dot(jnp.exp2((s - mx).astype(BF)), jnp.where(own, vcat, one), preferred_element_type=F32)
               for s, mx, own in zip(scores, mxs, (left_kv, ~left_kv))]
        yield
        den = pltpu.roll(jnp.where(left, pvs[1], pvs[0]), HEAD, 1)
        o_ref[0, 0, rows, cols] = (jnp.where(left, pvs[0], pvs[1]) / den).astype(BF)
        return (jnp.where(left, mxs[0], mxs[1]) + jnp.log2(den)) * LN2

    n_pairs = q_ref.shape[3] // LANES
    n_sub = q_ref.shape[2] // blk
    lses = _interleave([head_pair(sub, j) for sub in range(n_sub) for j in range(n_pairs)])
    for sub in range(n_sub):
        lse_tile = jnp.zeros((blk, LANES), F32)
        for j in range(n_pairs):
            lse_tile = jnp.where((lane & (HEAD - 1)) == j, lses[sub * n_pairs + j], lse_tile)
        l_ref[0, 0, sub * blk:(sub + 1) * blk, :] = lse_tile


def _attn(q, k, v):
    b, dilation, ln, d = q.shape
    blk = ATT_BLOCK
    step = ATT_STEP * blk
    cur = pl.BlockSpec((1, 1, step, d), lambda bi, r, n: (bi, r, n, 0))
    prev = pl.BlockSpec((1, 1, blk, d), lambda bi, r, n: (bi, r, jnp.maximum(ATT_STEP * n - 1, 0), 0))
    return pl.pallas_call(
        _attn_body,
        grid=(b, dilation, ln // step),
        in_specs=[cur, prev, cur, prev, cur],
        out_specs=[cur, pl.BlockSpec((1, 1, step, LANES), lambda bi, r, n: (bi, r, n, 0))],
        out_shape=[jax.ShapeDtypeStruct((b, dilation, ln, d), BF),
                   jax.ShapeDtypeStruct((b, dilation, ln, LANES), F32)],
        compiler_params=_params(("parallel", "parallel", "arbitrary")),
        name=f"attn_d{dilation}",
    )(q, k, k, v, v)


def _rotary_columns(w):
    k, n = w.shape
    half = HEAD // 2
    return w.reshape(k, n // LANES, 2, 2, half).transpose(0, 1, 3, 2, 4).reshape(k, n)


def _rope_tables(seq, gains, scale):
    half = HEAD // 2
    inv = ROPE_THETA ** (-jnp.arange(0, HEAD, 2, dtype=F32) / HEAD)
    ang = jnp.arange(seq, dtype=F32)[:, None] * inv[None, :]
    cos, sin = jnp.cos(ang), jnp.sin(ang)
    cos_t = jnp.concatenate([cos, cos, cos, cos], axis=1)
    sin_t = jnp.concatenate([-sin, -sin, sin, sin], axis=1)
    gains = gains.astype(F32) * scale
    lo, hi = gains[:, :half], gains[:, half:]
    own = jnp.concatenate([lo, lo, hi, hi], axis=1)
    partner = jnp.concatenate([hi, hi, lo, lo], axis=1)
    return cos_t[None] * own[:, None, :], sin_t[None] * partner[:, None, :]


def kernel(x, a_norm_g, a_mu, a_w_r, a_w_k, a_w_v, a_w0, a_w1, a_w2, a_a0, a_a1, a_a2, a_v0, a_v1, a_v2, a_g1, a_g2, a_k_k, a_k_a, a_r_k, a_lnx_g, a_lnx_b, a_w_o, kv_norm_g, w_kv, k_norm_g, b_norm_g, b_w_q, b_q_norm_g, b_w_o, f_norm_g, f_w_gu, f_w_down):
    bsz, seq, d = x.shape
    assert d == D_MODEL and seq % (ATT_STEP * ATT_BLOCK * DILATED_GROUPS[-1][1]) == 0
    assert seq % ROW_TILE == 0 and seq % WKV_TILE == 0
    n_a = a_norm_g.shape[0]
    n_b = b_norm_g.shape[0]
    row1 = lambda t: t.reshape(1, -1).astype(F32)
    bf = lambda t: t.astype(BF)

    seg_r, seg_c = _iota((SEG_WIDTH, SEG_WIDTH), 0), _iota((SEG_WIDTH, SEG_WIDTH), 1)
    seg = jnp.where(((seg_r >> 7) == (seg_c >> 7)) & (((seg_r ^ seg_c) & (HEAD // 2)) == 0),
                    1.0, 0.0).astype(BF)
    ex_row, ex_col = _iota((LANES, d), 0), _iota((LANES, d), 1)
    ex_head = 2 * (ex_row & (HEAD - 1)) + (ex_row >> 6)
    ex = jnp.where(((ex_row & (HEAD - 1)) < N_PAIRS) & (ex_head == (ex_col >> 6)), 1.0, 0.0).astype(BF)
    expand = jnp.concatenate([ex, ex], axis=0)

    x2d = x.reshape(bsz * seq, d)
    v_first = None
    for i in range(n_a):
        vmix = None if i == 0 else (row1(a_v0[i - 1]), bf(a_v1[i - 1]), bf(a_v2[i - 1]))
        mu = jnp.pad(a_mu[i], ((0, 2), (0, 0)))
        r, k, v, ld, al, g = _rwkv_proj(
            x2d, v_first, seq, row1(a_norm_g[i]), mu, bf(a_w_r[i]), bf(a_w_k[i]), bf(a_w_v[i]),
            row1(a_w0[i]), bf(a_w1[i]), bf(a_w2[i]), row1(a_a0[i]), bf(a_a1[i]), bf(a_a2[i]),
            vmix, bf(a_g1[i]), bf(a_g2[i]))
        if i == 0:
            v_first = v
        to3 = lambda t: t.reshape(bsz, seq, d)
        y = _wkv(to3(r), to3(k), to3(v), to3(ld), to3(al), to3(g), row1(a_k_k[i]), row1(a_k_a[i]),
                 row1(a_r_k[i]), row1(a_lnx_g[i]), row1(a_lnx_b[i]))
        x2d = _oproj_ffn(x2d, y.reshape(bsz * seq, d), bf(a_w_o[i]), row1(f_norm_g[i]),
                         bf(f_w_gu[i]), bf(f_w_down[i]))

    w_kv_r = jnp.concatenate([_rotary_columns(bf(w_kv[:, :Q_WIDTH])), bf(w_kv[:, Q_WIDTH:])], axis=1)
    kvs = _head_proj(x2d, bsz, seq, row1(kv_norm_g), w_kv_r, Q_WIDTH,
                     *_rope_tables(seq, k_norm_g, 1.0), seg)
    for j in range(n_b):
        qs = _head_proj(x2d, bsz, seq, row1(b_norm_g[j]), _rotary_columns(bf(b_w_q[j])), Q_WIDTH,
                        *_rope_tables(seq, b_q_norm_g[j], HEAD ** -0.5 * LOG2E), seg)
        outs, lses = [], []
        for gi in range(N_GROUPS):
            o, lse = _attn(qs[gi], kvs[gi], kvs[N_GROUPS + gi])
            outs.append(o)
            lses.append(lse)
        layer = n_a + j
        x2d = _merge_oproj_ffn(x2d, seq, outs, lses, expand, bf(b_w_o[j]), row1(f_norm_g[layer]),
                               bf(f_w_gu[layer]), bf(f_w_down[layer]))
    return x2d.reshape(bsz, seq, d)
```
